```python
import math
import jax
import jax.numpy as jnp
from jax import lax
import numpy as np

D_MODEL = 1024
BATCH = 8
SEQ = 2048
DEPTH = 4
DEC_BATCH = 32
DEC_SEQ = 4
PAST_LEN = 8192
PAGE_SIZE = 128

HEAD_DIM = 64
MIX_WIDTH = D_MODEL
N_GROUPS = 4
GROUP_WIDTH = MIX_WIDTH // N_GROUPS
GROUP_HEADS = GROUP_WIDTH // HEAD_DIM
FOX_HEADS = GROUP_HEADS
SB_HEADS = GROUP_HEADS
GDN_HEADS = GROUP_HEADS
GMLP_GROUPS = GROUP_HEADS
MIX_HEADS = FOX_HEADS + SB_HEADS + GDN_HEADS + GMLP_GROUPS
FOX_WIDTH = FOX_HEADS * HEAD_DIM
SB_WIDTH = SB_HEADS * HEAD_DIM
GDN_WIDTH = GDN_HEADS * HEAD_DIM
GMLP_WIDTH = GMLP_GROUPS * HEAD_DIM
GDN_CONV_DIM = 3 * GDN_WIDTH
CONV_WIDTH = 4
GDN_CHUNK = 64
QUERY_BLOCK = 128
MLP_CHUNK = 128
D_FF = 4 * D_MODEL
N_MOD = 6
FORGET_BIAS_INIT = 4.0
EPS = 1e-6
IN_WIDTHS = (FOX_WIDTH, FOX_WIDTH, FOX_WIDTH, FOX_HEADS,
             SB_WIDTH, SB_WIDTH, SB_WIDTH,
             GDN_CONV_DIM, GDN_HEADS, GDN_HEADS, GDN_WIDTH,
             2 * GMLP_WIDTH)
IN_WIDTH = sum(IN_WIDTHS)

kernel_name = 'hymba_fox_stickbreak_gdn_gmlp_step'


def split_points():
    return [int(v) for v in np.cumsum(IN_WIDTHS)[:-1]]


def rms_norm(x, g):
    xf = x.astype(jnp.float32)
    y = xf * lax.rsqrt(jnp.mean(xf * xf, axis=-1, keepdims=True) + EPS)
    return (y * g.astype(jnp.float32)).astype(x.dtype)


def l2_normalize(x):
    return x * lax.rsqrt(jnp.sum(x * x, axis=-1, keepdims=True) + EPS)


def sweep_query_blocks(block_fn, q_side):
    n_q = q_side[0].shape[1]
    qb = min(QUERY_BLOCK, n_q)
    n_blocks = -(-n_q // qb)
    pad = n_blocks * qb - n_q

    def to_blocks(a):
        a = jnp.pad(a, [(0, 0), (0, pad)] + [(0, 0)] * (a.ndim - 2))
        a = a.reshape((a.shape[0], n_blocks, qb) + a.shape[2:])
        return jnp.moveaxis(a, 1, 0)

    starts = jnp.arange(n_blocks, dtype=jnp.int32) * qb
    out = lax.map(lambda args: block_fn(*args), (starts,) + tuple(to_blocks(a) for a in q_side))
    out = jnp.moveaxis(out, 0, 1)
    out = out.reshape((out.shape[0], n_blocks * qb) + out.shape[3:])
    return out[:, :n_q]


def fox_attention(q, k, v, logf):
    n_q, n_k = q.shape[1], k.shape[1]
    offset = n_k - n_q
    F = lax.cumsum(logf, axis=1)
    F_key = jnp.transpose(F, (0, 2, 1))[:, :, None, :]
    k_pos = jnp.arange(n_k)
    scale = HEAD_DIM ** -0.5

    def block(start, q_blk, F_blk):
        q_pos = offset + start + jnp.arange(q_blk.shape[1])
        logits = jnp.einsum('bqhd,bkhd->bhqk', q_blk, k, preferred_element_type=jnp.float32) * scale
        logits = logits + jnp.transpose(F_blk, (0, 2, 1))[..., None] - F_key
        logits = jnp.where(k_pos[None, :] <= q_pos[:, None], logits, -jnp.inf)
        probs = jax.nn.softmax(logits, axis=-1)
        return jnp.einsum('bhqk,bkhd->bqhd', probs.astype(v.dtype), v)

    return sweep_query_blocks(block, (q, F[:, offset:]))


def stick_breaking_attention(q, k, v):
    n_q, n_k = q.shape[1], k.shape[1]
    offset = n_k - n_q
    k_pos = jnp.arange(n_k)
    scale = HEAD_DIM ** -0.5

    def block(start, q_blk):
        q_pos = offset + start + jnp.arange(q_blk.shape[1])
        z = jnp.einsum('bqhd,bkhd->bhqk', q_blk, k, preferred_element_type=jnp.float32) * scale
        mask = k_pos[None, :] < q_pos[:, None]
        log_keep = jnp.where(mask, jax.nn.log_sigmoid(-z), 0.0)
        after = lax.cumsum(log_keep, axis=log_keep.ndim - 1, reverse=True) - log_keep
        weights = jnp.where(mask, jnp.exp(jax.nn.log_sigmoid(z) + after), 0.0)
        return jnp.einsum('bhqk,bkhd->bqhd', weights.astype(v.dtype), v)

    return sweep_query_blocks(block, (q,))


def causal_conv(x, buf, w):
    n_t = x.shape[1]
    xp = jnp.concatenate([buf.astype(x.dtype), x], axis=1)
    y = xp[:, 0:n_t] * w[0]
    for i in range(1, CONV_WIDTH):
        y = y + xp[:, i:i + n_t] * w[i]
    return y, xp[:, xp.shape[1] - (CONV_WIDTH - 1):]


def gated_delta_chunked(q, k, v, g, beta, s0):
    n_b, n_t, n_h, d_k = q.shape
    d_v = v.shape[-1]
    n_c = -(-n_t // GDN_CHUNK)
    pad = n_c * GDN_CHUNK - n_t

    def chunks(a):
        a = jnp.pad(a, [(0, 0), (0, pad)] + [(0, 0)] * (a.ndim - 2))
        a = a.reshape((n_b, n_c, GDN_CHUNK) + a.shape[2:])
        return jnp.moveaxis(a, (1, 2), (0, 3))

    qc, kc, vc, gc, bc = chunks(q), chunks(k), chunks(v), chunks(g), chunks(beta)
    G = lax.cumsum(gc, axis=gc.ndim - 1)
    idx = jnp.arange(GDN_CHUNK)
    incl = idx[:, None] >= idx[None, :]
    strict = idx[:, None] > idx[None, :]
    decay = jnp.exp(jnp.where(incl, G[..., :, None] - G[..., None, :], -jnp.inf))
    kb = kc * bc[..., None]
    m_low = jnp.where(strict, jnp.einsum('nbhid,nbhjd->nbhij', kb, kc) * decay, 0.0)
    rhs = jnp.concatenate([vc * bc[..., None], kb * jnp.exp(G)[..., None]], axis=-1)
    sol = lax.linalg.triangular_solve(m_low + jnp.eye(GDN_CHUNK, dtype=m_low.dtype), rhs,
                                      left_side=True, lower=True, unit_diagonal=True)
    u, w = sol[..., :d_v], sol[..., d_v:]
    a_qk = jnp.einsum('nbhid,nbhjd->nbhij', qc, kc) * decay

    def step(s, xs):
        q_i, k_i, u_i, w_i, g_i, a_i = xs
        v_new = u_i - jnp.einsum('bhck,bhkv->bhcv', w_i, s)
        o_i = (jnp.einsum('bhck,bhkv->bhcv', q_i * jnp.exp(g_i)[..., None], s)
               + jnp.einsum('bhij,bhjv->bhiv', a_i, v_new))
        g_last = g_i[..., -1:]
        s = s * jnp.exp(g_last)[..., None] + jnp.einsum(
            'bhck,bhcv->bhkv', k_i * jnp.exp(g_last - g_i)[..., None], v_new)
        return s, o_i

    s_final, o = lax.scan(step, s0, (qc, kc, u, w, G, a_qk))
    o = jnp.moveaxis(o, (0, 3), (1, 2)).reshape(n_b, n_c * GDN_CHUNK, n_h, d_v)[:, :n_t]
    return o, s_final


def gated_deltanet(qkv, a, b, s0, conv_buf, conv_w, a_log, dt_bias):
    n_b, n_t, _ = qkv.shape
    y, new_buf = causal_conv(qkv, conv_buf, conv_w)
    y = jax.nn.silu(y.astype(jnp.float32))
    q, k, v = jnp.split(y, 3, axis=-1)
    q = l2_normalize(q.reshape(n_b, n_t, GDN_HEADS, HEAD_DIM)) * (HEAD_DIM ** -0.5)
    k = l2_normalize(k.reshape(n_b, n_t, GDN_HEADS, HEAD_DIM))
    v = v.reshape(n_b, n_t, GDN_HEADS, HEAD_DIM)
    beta = jax.nn.sigmoid(b.astype(jnp.float32))
    g = -jnp.exp(a_log.astype(jnp.float32)) * jax.nn.softplus(a.astype(jnp.float32) + dt_bias.astype(jnp.float32))
    o, s_new = gated_delta_chunked(q, k, v, g, beta, s0.astype(jnp.float32))
    return o, s_new, new_buf


def chunk_spatial_gating(uv, v_norm_g, w_s, b_s):
    n_b, n_t, _ = uv.shape
    u, v = jnp.split(jax.nn.gelu(uv), 2, axis=-1)
    v = rms_norm(v, v_norm_g)
    n_c = -(-n_t // MLP_CHUNK)
    pad = n_c * MLP_CHUNK - n_t
    vc = jnp.pad(v, ((0, 0), (0, pad), (0, 0))).reshape(n_b, n_c, MLP_CHUNK, GMLP_GROUPS, HEAD_DIM)
    tri = jnp.tril(jnp.ones((MLP_CHUNK, MLP_CHUNK), dtype=bool))
    w = jnp.where(tri, w_s, 0.0).astype(vc.dtype)
    mixed = jnp.einsum('gij,bnjgd->bnigd', w, vc) + jnp.transpose(b_s).astype(vc.dtype)[None, None, :, :, None]
    mixed = mixed.reshape(n_b, n_c * MLP_CHUNK, GMLP_WIDTH)[:, :n_t]
    return u * mixed, v


def token_mixing(h, p, past):
    fox_k_past, fox_v_past, fox_logf_past, sb_k_past, sb_v_past, gdn_s0, gdn_buf = past
    n_b, n_t, _ = h.shape
    proj = h @ p['w_in']
    (fq, fk, fv, ff, sq, sk, sv, gqkv, ga, gb, ggate, guv) = jnp.split(proj, split_points(), axis=-1)

    def to_heads(a):
        return a.reshape(n_b, n_t, -1, HEAD_DIM)

    fq, fk, fv, sq, sk, sv = (to_heads(a) for a in (fq, fk, fv, sq, sk, sv))
    logf = jax.nn.log_sigmoid(ff.astype(jnp.float32) + p['b_forget'].astype(jnp.float32))

    o_fox = fox_attention(fq,
                          jnp.concatenate([fox_k_past.astype(fk.dtype), fk], axis=1),
                          jnp.concatenate([fox_v_past.astype(fv.dtype), fv], axis=1),
                          jnp.concatenate([fox_logf_past.astype(jnp.float32), logf], axis=1))
    o_sb = stick_breaking_attention(sq,
                                    jnp.concatenate([sb_k_past.astype(sk.dtype), sk], axis=1),
                                    jnp.concatenate([sb_v_past.astype(sv.dtype), sv], axis=1))
    o_gdn, s_new, buf_new = gated_deltanet(gqkv, ga, gb, gdn_s0, gdn_buf, p['gdn_conv_w'],
                                           p['gdn_a_log'], p['gdn_dt_bias'])
    o_gmlp, v_rows = chunk_spatial_gating(guv, p['gmlp_v_norm_g'], p['gmlp_w_s'], p['gmlp_b_s'])

    heads = jnp.concatenate([o_fox, o_sb, o_gdn.astype(h.dtype), to_heads(o_gmlp)], axis=2)
    heads = rms_norm(heads, p['head_norm_g'])
    n_att = FOX_HEADS + SB_HEADS
    heads = jnp.concatenate([heads[:, :, :n_att],
                             heads[:, :, n_att:n_att + GDN_HEADS] * jax.nn.silu(to_heads(ggate)),
                             heads[:, :, n_att + GDN_HEADS:]], axis=2)
    out = heads.reshape(n_b, n_t, MIX_WIDTH) @ p['w_out']
    new_state = (fk, fv, logf.astype(h.dtype), sk, sv, s_new.astype(h.dtype), buf_new, v_rows)
    return out, new_state


def decoder_layer(x, c, p, past):
    mod = (jax.nn.silu(c) @ p['ada_w'] + p['ada_b'])[:, None, :]
    shift1, scale1, gate1, shift2, scale2, gate2 = jnp.split(mod, N_MOD, axis=-1)
    h = rms_norm(x, p['norm_g'][0]) * (1 + scale1) + shift1
    m, new_state = token_mixing(h, p, past)
    x = x + gate1 * rms_norm(m, p['norm_g'][1])
    h = rms_norm(x, p['norm_g'][2]) * (1 + scale2) + shift2
    f = jnp.square(jax.nn.relu(h @ p['w_ff1'])) @ p['w_ff2']
    x = x + gate2 * rms_norm(f, p['norm_g'][3])
    return x, new_state


def setup_inputs(seed: int = 0):
    key = jax.random.key(seed)
    keys = iter(jax.random.split(key, 40))

    def nrm(shape, scale):
        return jax.random.normal(next(keys), shape, jnp.float32) * scale

    n_pages = PAST_LEN // PAGE_SIZE
    n_used = DEC_BATCH * n_pages
    n_phys = n_used + max(1, n_used // 4)
    page_table = jax.random.permutation(next(keys), n_phys)[:n_used].reshape(DEC_BATCH, n_pages).astype(jnp.int32)
    dt_init = jnp.exp(jax.random.uniform(next(keys), (DEPTH, GDN_HEADS), jnp.float32,
                                         math.log(1e-3), math.log(1e-1)))
    return {
        'x_prompt': nrm((BATCH, SEQ, D_MODEL), 1.0),
        'x_sample': nrm((DEC_BATCH, DEC_SEQ, D_MODEL), 1.0),
        'cache_fox_k': nrm((DEPTH, n_phys, PAGE_SIZE, FOX_HEADS, HEAD_DIM), 1.0),
        'cache_fox_v': nrm((DEPTH, n_phys, PAGE_SIZE, FOX_HEADS, HEAD_DIM), 1.0),
        'cache_fox_logf': jax.nn.log_sigmoid(FORGET_BIAS_INIT + nrm((DEPTH, n_phys, PAGE_SIZE, FOX_HEADS), 1.0)),
        'cache_sb_k': nrm((DEPTH, n_phys, PAGE_SIZE, SB_HEADS, HEAD_DIM), 1.0),
        'cache_sb_v': nrm((DEPTH, n_phys, PAGE_SIZE, SB_HEADS, HEAD_DIM), 1.0),
        'state_gdn': nrm((DEPTH, DEC_BATCH, GDN_HEADS, HEAD_DIM, HEAD_DIM), 0.3),
        'state_gdn_conv': nrm((DEPTH, DEC_BATCH, CONV_WIDTH - 1, GDN_CONV_DIM), 1.0),
        'page_table': page_table,
        'c_prompt': nrm((BATCH, D_MODEL), 1.0),
        'c_sample': nrm((DEC_BATCH, D_MODEL), 1.0),
        'ada_w': nrm((DEPTH, D_MODEL, N_MOD * D_MODEL), 0.5 * D_MODEL ** -0.5),
        'ada_b': nrm((DEPTH, N_MOD * D_MODEL), 0.02),
        'norm_g': 1.0 + nrm((DEPTH, 4, D_MODEL), 0.05),
        'w_in': nrm((DEPTH, D_MODEL, IN_WIDTH), D_MODEL ** -0.5),
        'b_forget': FORGET_BIAS_INIT + nrm((DEPTH, FOX_HEADS), 0.1),
        'gdn_conv_w': nrm((DEPTH, CONV_WIDTH, GDN_CONV_DIM), CONV_WIDTH ** -0.5),
        'gdn_a_log': jnp.log(jax.random.uniform(next(keys), (DEPTH, GDN_HEADS), jnp.float32, 1.0, 16.0)),
        'gdn_dt_bias': dt_init + jnp.log(-jnp.expm1(-dt_init)),
        'gmlp_v_norm_g': 1.0 + nrm((DEPTH, GMLP_WIDTH), 0.05),
        'gmlp_w_s': nrm((DEPTH, GMLP_GROUPS, MLP_CHUNK, MLP_CHUNK), MLP_CHUNK ** -0.5),
        'gmlp_b_s': 1.0 + nrm((DEPTH, GMLP_GROUPS, MLP_CHUNK), 0.1),
        'head_norm_g': 1.0 + nrm((DEPTH, MIX_HEADS, HEAD_DIM), 0.05),
        'w_out': nrm((DEPTH, MIX_WIDTH, D_MODEL), MIX_WIDTH ** -0.5),
        'w_ff1': nrm((DEPTH, D_MODEL, D_FF), D_MODEL ** -0.5),
        'w_ff2': nrm((DEPTH, D_FF, D_MODEL), D_FF ** -0.5),
    }


def reference(x_prompt, x_sample, cache_fox_k, cache_fox_v, cache_fox_logf, cache_sb_k, cache_sb_v,
              state_gdn, state_gdn_conv, page_table, c_prompt, c_sample,
              ada_w, ada_b, norm_g, w_in, b_forget, gdn_conv_w, gdn_a_log, gdn_dt_bias,
              gmlp_v_norm_g, gmlp_w_s, gmlp_b_s, head_norm_g, w_out, w_ff1, w_ff2):
    n_prompt = x_prompt.shape[0]
    n_dec = x_sample.shape[0]
    n_past = page_table.shape[1] * PAGE_SIZE
    dt = x_prompt.dtype

    def gather_pages(cache_layer):
        rows = cache_layer[page_table]
        return rows.reshape((n_dec, n_past) + cache_layer.shape[2:])

    prompt_past = (jnp.zeros((n_prompt, 0, FOX_HEADS, HEAD_DIM), dt),
                   jnp.zeros((n_prompt, 0, FOX_HEADS, HEAD_DIM), dt),
                   jnp.zeros((n_prompt, 0, FOX_HEADS), jnp.float32),
                   jnp.zeros((n_prompt, 0, SB_HEADS, HEAD_DIM), dt),
                   jnp.zeros((n_prompt, 0, SB_HEADS, HEAD_DIM), dt),
                   jnp.zeros((n_prompt, GDN_HEADS, HEAD_DIM, HEAD_DIM), jnp.float32),
                   jnp.zeros((n_prompt, CONV_WIDTH - 1, GDN_CONV_DIM), dt))

    y_prompt, y_sample = x_prompt, x_sample
    new_prompt, new_sample = [], []
    for l in range(DEPTH):
        params = dict(ada_w=ada_w[l], ada_b=ada_b[l], norm_g=norm_g[l], w_in=w_in[l],
                      b_forget=b_forget[l], gdn_conv_w=gdn_conv_w[l], gdn_a_log=gdn_a_log[l],
                      gdn_dt_bias=gdn_dt_bias[l], gmlp_v_norm_g=gmlp_v_norm_g[l],
                      gmlp_w_s=gmlp_w_s[l], gmlp_b_s=gmlp_b_s[l], head_norm_g=head_norm_g[l],
                      w_out=w_out[l], w_ff1=w_ff1[l], w_ff2=w_ff2[l])
        sample_past = (gather_pages(cache_fox_k[l]), gather_pages(cache_fox_v[l]),
                       gather_pages(cache_fox_logf[l]), gather_pages(cache_sb_k[l]),
                       gather_pages(cache_sb_v[l]), state_gdn[l], state_gdn_conv[l])
        y_prompt, st_p = decoder_layer(y_prompt, c_prompt, params, prompt_past)
        y_sample, st_s = decoder_layer(y_sample, c_sample, params, sample_past)
        new_prompt.append(st_p)
        new_sample.append(st_s)

    def stacked(states, i):
        return jnp.stack([s[i] for s in states])

    return (y_prompt, y_sample,
            stacked(new_prompt, 0), stacked(new_prompt, 1), stacked(new_prompt, 2),
            stacked(new_prompt, 3), stacked(new_prompt, 4), stacked(new_prompt, 5), stacked(new_prompt, 6),
            stacked(new_sample, 0), stacked(new_sample, 1), stacked(new_sample, 2),
            stacked(new_sample, 3), stacked(new_sample, 4), stacked(new_sample, 5), stacked(new_sample, 6),
            stacked(new_sample, 7))
```

```python
import functools
import math

import jax
import jax.numpy as jnp
from jax import lax
from jax.experimental import pallas as pl
from jax.experimental.pallas import tpu as pltpu

F32 = jnp.float32
BF16 = jnp.bfloat16

HEAD_DIM = 64
GROUP_HEADS = 4
GROUP_WIDTH = GROUP_HEADS * HEAD_DIM
CONV_WIDTH = 4
GDN_CONV_DIM = 3 * GROUP_WIDTH
GDN_CHUNK = 64
GDN_CHUNK_SMALL = 16
MLP_CHUNK = 128
PAGE_SIZE = 128
N_MOD = 6
EPS = 1e-6
QK_SCALE = HEAD_DIM ** -0.5
NEG_BIG = -1e30

SM_LOGF = 0
SM_G = 4
SM_BETA = 8
SM_CUM = 12
SMALL_W = 128

C_FQ, C_FK, C_FV, C_SQ, C_SK, C_SV = (i * GROUP_WIDTH for i in range(6))
C_GQKV = 6 * GROUP_WIDTH
C_GGATE = C_GQKV + GDN_CONV_DIM
C_GUV = C_GGATE + GROUP_WIDTH
C_SMALL = C_GUV + 2 * GROUP_WIDTH
IN_PAD_WIDTH = C_SMALL + SMALL_W

VMEM_LIMIT = 56 * 1024 * 1024


def _cparams(sem):
    return pltpu.CompilerParams(dimension_semantics=sem, vmem_limit_bytes=VMEM_LIMIT)


def _dot(a, b):
    return jnp.dot(a, b, preferred_element_type=F32)


def _dot_nt(a, b):
    return lax.dot_general(a, b, (((1,), (1,)), ((), ())), preferred_element_type=F32)


def _split2(a):
    hi = a.astype(BF16)
    lo = (a - hi.astype(F32)).astype(BF16)
    return hi, lo


def _split3(a):
    hi = a.astype(BF16)
    r = a - hi.astype(F32)
    mid = r.astype(BF16)
    lo = (r - mid.astype(F32)).astype(BF16)
    return hi, mid, lo


def _dot3(dot, a, b):
    ah, al = _split2(a)
    bh, bl = _split2(b)
    return dot(ah, bh) + (dot(ah, bl) + dot(al, bh))


def _dot_exact_l(dot, a01, b, pieces=3):
    parts = _split3(b) if pieces == 3 else _split2(b)
    out = dot(a01, parts[0])
    for p in parts[1:]:
        out = out + dot(a01, p)
    return out


def _dot_exact_r(dot, a, b01, pieces=3):
    parts = _split3(a) if pieces == 3 else _split2(a)
    out = dot(parts[0], b01)
    for p in parts[1:]:
        out = out + dot(p, b01)
    return out


def _softplus_parts(x):
    return jnp.log1p(jnp.exp(-jnp.abs(x)))


def _sigmoid(x):
    return 1.0 / (1.0 + jnp.exp(-x))


def _silu(x):
    return x * _sigmoid(x)


def _rms(x, g):
    return x * lax.rsqrt(jnp.mean(x * x, axis=-1, keepdims=True) + EPS) * g


def _iota(shape, dim):
    return lax.broadcasted_iota(jnp.int32, shape, dim)


def _tri01(n, strict, lower, dtype=BF16):
    r, c = _iota((n, n), 0), _iota((n, n), 1)
    if lower:
        m = (r > c) if strict else (r >= c)
    else:
        m = (r < c) if strict else (r <= c)
    return jnp.where(m, 1.0, 0.0).astype(dtype)


def _group_ones(n, group):
    r, c = _iota((n, n), 0), _iota((n, n), 1)
    return jnp.where((r // group) == (c // group), 1.0, 0.0).astype(BF16)


def _head_norm_lanes(x, gain):
    ones = _group_ones(x.shape[-1], HEAD_DIM)
    ms = _dot_exact_r(_dot, x * x, ones) * (1.0 / HEAD_DIM)
    return x * lax.rsqrt(ms + EPS) * gain


def _mod_kernel(c_ref, w_ref, b_ref, o_ref):
    a = _silu(c_ref[...]).astype(BF16)
    o_ref[0] = _dot(a, w_ref[0].astype(BF16)) + b_ref[0]


def _modulation(c_all, ada_w, ada_b):
    depth, d, n = ada_w.shape
    rows = c_all.shape[0]
    tn = 1536 if n % 1536 == 0 else n
    return pl.pallas_call(
        _mod_kernel,
        out_shape=jax.ShapeDtypeStruct((depth, rows, n), F32),
        grid=(depth, n // tn),
        in_specs=[pl.BlockSpec((rows, d), lambda l, j: (0, 0)),
                  pl.BlockSpec((1, d, tn), lambda l, j: (l, 0, j)),
                  pl.BlockSpec((1, 1, tn), lambda l, j: (l, 0, j))],
        out_specs=pl.BlockSpec((1, rows, tn), lambda l, j: (l, 0, j)),
        compiler_params=_cparams(("arbitrary", "arbitrary")),
        name="adaln_mod",
    )(c_all, ada_w, ada_b.reshape(depth, 1, n))


def _inproj_kernel(x_ref, sh_ref, sc_ref, g_ref, w_ref, wt_ref, sp_ref,
                   fq, fk, fv, sq, sk, sv, gqkv, ggate, guv, small, carry_ref, *, cumsum, kv_transposed):
    x = x_ref[0]
    h = _rms(x, g_ref[...]) * (1.0 + sc_ref[0]) + sh_ref[0]
    hb = h.astype(BF16)

    def seg(off, n):
        return _dot(hb, w_ref[:, off:off + n])

    def seg_t(i):
        return _dot_nt(wt_ref[i * GROUP_WIDTH:(i + 1) * GROUP_WIDTH, :], hb)

    fq[0] = seg(C_FQ, GROUP_WIDTH)
    sq[0] = seg(C_SQ, GROUP_WIDTH)
    if kv_transposed:
        fk[0] = seg_t(0)
        fv[0] = seg_t(1)
        sk[0] = seg_t(2)
        sv[0] = seg_t(3)
    else:
        fk[0] = seg(C_FK, GROUP_WIDTH)
        fv[0] = seg(C_FV, GROUP_WIDTH)
        sk[0] = seg(C_SK, GROUP_WIDTH)
        sv[0] = seg(C_SV, GROUP_WIDTH)
    gqkv[0] = seg(C_GQKV, GDN_CONV_DIM)
    ggate[0] = seg(C_GGATE, GROUP_WIDTH)
    guv[0] = seg(C_GUV, 2 * GROUP_WIDTH)

    z = seg(C_SMALL, SMALL_W)
    tm = z.shape[0]
    lane = _iota(z.shape, 1)
    zz = z + sp_ref[0:1, :]
    t = _softplus_parts(zz)
    logf = jnp.minimum(zz, 0.0) - t
    gdec = -jnp.exp(sp_ref[1:2, :]) * (jnp.maximum(zz, 0.0) + t)
    beta = _sigmoid(z)
    out = jnp.where(lane < SM_G, logf, jnp.where(lane < SM_BETA, gdec, jnp.where(lane < SM_CUM, beta, 0.0)))
    if cumsum:
        @pl.when(pl.program_id(1) == 0)
        def _():
            carry_ref[...] = jnp.zeros_like(carry_ref)

        lf = jnp.where((lane >= SM_CUM) & (lane < SM_CUM + GROUP_HEADS), logf, 0.0)
        cum = _dot_exact_l(_dot, _tri01(tm, strict=False, lower=True), lf) + carry_ref[0:1, :]
        carry_ref[0:1, :] = cum[tm - 1:tm, :]
        out = out + cum
    small[0] = out


def _in_projection(x, shift, scale, g, w, wt, sp, *, cumsum, kv_transposed):
    nb, t, d = x.shape
    tm = min(512, t)
    tmod = shift.shape[1]
    bm = tm if tmod == t else 1
    mod_spec = pl.BlockSpec((1, bm, d), (lambda b, i: (b, i, 0)) if tmod == t else (lambda b, i: (b, 0, 0)))
    gw = GROUP_WIDTH
    row = lambda wd: (jax.ShapeDtypeStruct((nb, t, wd), F32), pl.BlockSpec((1, tm, wd), lambda b, i: (b, i, 0)))
    col = (jax.ShapeDtypeStruct((nb, gw, t), F32), pl.BlockSpec((1, gw, tm), lambda b, i: (b, 0, i)))
    kv = col if kv_transposed else row(gw)
    outs = [row(gw), kv, kv, row(gw), kv, kv, row(GDN_CONV_DIM), row(gw), row(2 * gw), row(SMALL_W)]
    return pl.pallas_call(
        functools.partial(_inproj_kernel, cumsum=cumsum, kv_transposed=kv_transposed),
        out_shape=[o[0] for o in outs],
        grid=(nb, t // tm),
        in_specs=[pl.BlockSpec((1, tm, d), lambda b, i: (b, i, 0)),
                  mod_spec, mod_spec,
                  pl.BlockSpec((1, d), lambda b, i: (0, 0)),
                  pl.BlockSpec((d, IN_PAD_WIDTH), lambda b, i: (0, 0)),
                  pl.BlockSpec((4 * gw, d), lambda b, i: (0, 0)),
                  pl.BlockSpec((8, SMALL_W), lambda b, i: (0, 0))],
        out_specs=[o[1] for o in outs],
        scratch_shapes=[pltpu.VMEM((8, SMALL_W), F32)],
        compiler_params=_cparams(("arbitrary", "arbitrary")),
        name="in_projection",
    )(x, shift, scale, g, w, wt, sp)


def _fox_kernel(q_ref, kt_ref, vt_ref, smq_ref, frow_ref, hg_ref, o_ref, ks_ref, vs_ref, *, tq):
    qi = pl.program_id(1)

    @pl.when(qi == 0)
    def _():
        ks_ref[...] = kt_ref[0].astype(BF16)
        vs_ref[...] = vt_ref[0].astype(BF16)

    q_all = q_ref[0]
    sm = smq_ref[0]
    row = _iota((tq, tq), 0)
    col = _iota((tq, tq), 1)
    outs = []
    for h in range(GROUP_HEADS):
        lo = h * HEAD_DIM
        qh = (q_all[:, lo:lo + HEAD_DIM] * QK_SCALE).astype(BF16)
        f_q = sm[:, SM_CUM + h:SM_CUM + h + 1]

        def step(j, carry, masked, h=h, lo=lo, qh=qh, f_q=f_q):
            m, l, acc = carry
            start = pl.multiple_of(j * tq, tq)
            kb = ks_ref[lo:lo + HEAD_DIM, pl.ds(start, tq)]
            vb = vs_ref[lo:lo + HEAD_DIM, pl.ds(start, tq)]
            f_k = frow_ref[0, h:h + 1, pl.ds(start, tq)]
            s = _dot(qh, kb) + (f_q - f_k)
            if masked:
                s = jnp.where(col <= row, s, NEG_BIG)
            m_new = jnp.maximum(m, jnp.max(s, axis=1, keepdims=True))
            alpha = jnp.exp(m - m_new)
            p = jnp.exp(s - m_new)
            l = alpha * l + jnp.sum(p, axis=1, keepdims=True)
            acc = alpha * acc + _dot_nt(p.astype(BF16), vb)
            return m_new, l, acc

        init = (jnp.full((tq, 1), NEG_BIG, F32), jnp.zeros((tq, 1), F32), jnp.zeros((tq, HEAD_DIM), F32))
        carry = lax.fori_loop(0, qi, functools.partial(step, masked=False), init)
        m, l, acc = step(qi, carry, True)
        o = acc / l
        outs.append(_rms(o, hg_ref[:, lo:lo + HEAD_DIM]))
    o_ref[0] = jnp.concatenate(outs, axis=1)


def _fox_prompt(q, kt, vt, small, frow, hg):
    nb, t, w = q.shape
    tq = min(256, t)
    return pl.pallas_call(
        functools.partial(_fox_kernel, tq=tq),
        out_shape=jax.ShapeDtypeStruct((nb, t, w), F32),
        grid=(nb, t // tq),
        in_specs=[pl.BlockSpec((1, tq, w), lambda b, i: (b, i, 0)),
                  pl.BlockSpec((1, w, t), lambda b, i: (b, 0, 0)),
                  pl.BlockSpec((1, w, t), lambda b, i: (b, 0, 0)),
                  pl.BlockSpec((1, tq, SMALL_W), lambda b, i: (b, i, 0)),
                  pl.BlockSpec((1, GROUP_HEADS, t), lambda b, i: (b, 0, 0)),
                  pl.BlockSpec((1, w), lambda b, i: (0, 0))],
        out_specs=pl.BlockSpec((1, tq, w), lambda b, i: (b, i, 0)),
        scratch_shapes=[pltpu.VMEM((w, t), BF16), pltpu.VMEM((w, t), BF16)],
        compiler_params=_cparams(("arbitrary", "arbitrary")),
        name="fox_prompt",
    )(q, kt, vt, small, frow, hg)


def _sb_kernel(q_ref, kt_ref, vt_ref, hg_ref, o_ref, ks_ref, vs_ref, *, tq):
    qi = pl.program_id(1)

    @pl.when(qi == 0)
    def _():
        ks_ref[...] = kt_ref[0].astype(BF16)
        vs_ref[...] = vt_ref[0].astype(BF16)

    q_all = q_ref[0]
    row = _iota((tq, tq), 0)
    col = _iota((tq, tq), 1)
    later = _tri01(tq, strict=True, lower=True)
    outs = []
    for h in range(GROUP_HEADS):
        lo = h * HEAD_DIM
        qh = (q_all[:, lo:lo + HEAD_DIM] * QK_SCALE).astype(BF16)

        def step(j, carry, masked, lo=lo, qh=qh):
            run, acc = carry
            start = pl.multiple_of(j * tq, tq)
            kb = ks_ref[lo:lo + HEAD_DIM, pl.ds(start, tq)]
            vb = vs_ref[lo:lo + HEAD_DIM, pl.ds(start, tq)]
            z = _dot(qh, kb)
            t = _softplus_parts(z)
            log_keep = -(jnp.maximum(z, 0.0) + t)
            if masked:
                log_keep = jnp.where(col < row, log_keep, 0.0)
            after = _dot_exact_r(_dot, log_keep, later, pieces=2) + run
            wgt = jnp.exp(log_keep + z + after)
            if masked:
                wgt = jnp.where(col < row, wgt, 0.0)
            acc = acc + _dot_nt(wgt.astype(BF16), vb)
            run = run + jnp.sum(log_keep, axis=1, keepdims=True)
            return run, acc

        init = (jnp.zeros((tq, 1), F32), jnp.zeros((tq, HEAD_DIM), F32))
        carry = step(qi, init, True)
        run, acc = lax.fori_loop(0, qi, lambda i, c: step(qi - 1 - i, c, False), carry)
        outs.append(_rms(acc, hg_ref[:, lo:lo + HEAD_DIM]))
    o_ref[0] = jnp.concatenate(outs, axis=1)


def _sb_prompt(q, kt, vt, hg):
    nb, t, w = q.shape
    tq = min(256, t)
    return pl.pallas_call(
        functools.partial(_sb_kernel, tq=tq),
        out_shape=jax.ShapeDtypeStruct((nb, t, w), F32),
        grid=(nb, t // tq),
        in_specs=[pl.BlockSpec((1, tq, w), lambda b, i: (b, i, 0)),
                  pl.BlockSpec((1, w, t), lambda b, i: (b, 0, 0)),
                  pl.BlockSpec((1, w, t), lambda b, i: (b, 0, 0)),
                  pl.BlockSpec((1, w), lambda b, i: (0, 0))],
        out_specs=pl.BlockSpec((1, tq, w), lambda b, i: (b, i, 0)),
        scratch_shapes=[pltpu.VMEM((w, t), BF16), pltpu.VMEM((w, t), BF16)],
        compiler_params=_cparams(("arbitrary", "arbitrary")),
        name="sb_prompt",
    )(q, kt, vt, hg)


def _unit_lower_inverse(m_low, c):
    r, cc = _iota((c, c), 0), _iota((c, c), 1)
    p = jnp.where(r == cc, 1.0, 0.0) - m_low
    mp = m_low
    for _ in range(int(math.log2(c)) - 1):
        mp = _dot3(_dot, mp, mp)
        p = p + _dot3(_dot, p, mp)
    return p


def _gdn_kernel(x_ref, sm_ref, gate_ref, cb_ref, s0_ref, cw_ref, hg_ref,
                o_ref, sn_ref, cbn_ref, tail_ref, st_ref, *, c, n_valid):
    ci = pl.program_id(1)
    n_chunks = pl.num_programs(1)

    @pl.when(ci == 0)
    def _():
        tail_ref[...] = cb_ref[0]
        st_ref[...] = s0_ref[0]

    x = x_ref[0]
    tail = tail_ref[...]
    rowi = _iota((c, GDN_CONV_DIM), 0)
    y = x * cw_ref[CONV_WIDTH - 1:CONV_WIDTH, :]
    tail_tiled = jnp.concatenate([tail] * (c // 8), axis=0)
    for s in range(1, CONV_WIDTH):
        shifted = jnp.where(rowi < s, pltpu.roll(tail_tiled, s, axis=0), pltpu.roll(x, s, axis=0))
        y = y + shifted * cw_ref[CONV_WIDTH - 1 - s:CONV_WIDTH - s, :]
    y = _silu(y)

    n_last = c if n_valid is None else n_valid
    if n_last % 8 == 0:
        new_tail = x[n_last - 8:n_last, :]
    else:
        cat = jnp.concatenate([tail, x[0:8, :]], axis=0)
        new_tail = pltpu.roll(cat, 16 - n_last, axis=0)[0:8, :]
    tail_ref[...] = x[c - 8:c, :]

    @pl.when(ci == n_chunks - 1)
    def _():
        cbn_ref[0] = new_tail

    sm = sm_ref[0]
    if n_valid is not None:
        valid = _iota((c, 1), 0) < n_valid
        sm = jnp.where(valid, sm, 0.0)
    lane = _iota(sm.shape, 1)
    g_l = jnp.where((lane >= SM_G) & (lane < SM_G + GROUP_HEADS), sm, 0.0)
    gcum = _dot_exact_l(_dot, _tri01(c, strict=False, lower=True), g_l)
    sel = jnp.where(_iota((8, SMALL_W), 1) == _iota((8, SMALL_W), 0) + SM_G, 1.0, 0.0).astype(BF16)
    grow = _dot_exact_l(_dot_nt, sel, gcum)
    ri, cj = _iota((c, c), 0), _iota((c, c), 1)
    incl = ri >= cj
    strict = ri > cj
    gate = gate_ref[0]
    outs = []
    for h in range(GROUP_HEADS):
        lo = h * HEAD_DIM
        qh = y[:, lo:lo + HEAD_DIM]
        kh = y[:, GROUP_WIDTH + lo:GROUP_WIDTH + lo + HEAD_DIM]
        vh = y[:, 2 * GROUP_WIDTH + lo:2 * GROUP_WIDTH + lo + HEAD_DIM]
        qh = qh * lax.rsqrt(jnp.sum(qh * qh, axis=-1, keepdims=True) + EPS) * QK_SCALE
        kh = kh * lax.rsqrt(jnp.sum(kh * kh, axis=-1, keepdims=True) + EPS)
        if n_valid is not None:
            qh = jnp.where(valid, qh, 0.0)
            kh = jnp.where(valid, kh, 0.0)
            vh = jnp.where(valid, vh, 0.0)
        gc = gcum[:, SM_G + h:SM_G + h + 1]
        gr = grow[h:h + 1, :]
        bh = sm[:, SM_BETA + h:SM_BETA + h + 1]
        decay = jnp.exp(jnp.where(incl, gc - gr, NEG_BIG))
        kb = kh * bh
        m_low = jnp.where(strict, _dot3(_dot_nt, kb, kh) * decay, 0.0)
        egc = jnp.exp(gc)
        rhs = jnp.concatenate([vh * bh, kb * egc], axis=1)
        sol = _dot3(_dot, _unit_lower_inverse(m_low, c), rhs)
        u = sol[:, :HEAD_DIM]
        w = sol[:, HEAD_DIM:]
        a_qk = _dot3(_dot_nt, qh, kh) * decay
        s = st_ref[h]
        v_new = u - _dot3(_dot, w, s)
        o = _dot3(_dot, qh * egc, s) + _dot3(_dot, a_qk, v_new)
        g_last = gc[c - 1:c, :]
        kdec_t = _dot_exact_l(_dot_nt, jnp.where(_iota((HEAD_DIM, HEAD_DIM), 0) == _iota((HEAD_DIM, HEAD_DIM), 1),
                                                 1.0, 0.0).astype(BF16),
                              kh * jnp.exp(g_last - gc))
        st_ref[h] = s * jnp.exp(g_last) + _dot3(_dot, kdec_t, v_new)
        o = _rms(o, hg_ref[:, lo:lo + HEAD_DIM]) * _silu(gate[:, lo:lo + HEAD_DIM])
        outs.append(o)
    o_ref[0] = jnp.concatenate(outs, axis=1)

    @pl.when(ci == n_chunks - 1)
    def _():
        sn_ref[0] = st_ref[...]


def _gated_deltanet(gqkv, small, ggate, conv_buf8, s0, conv_w, hg, *, c, n_valid):
    nb, t, _ = gqkv.shape
    return pl.pallas_call(
        functools.partial(_gdn_kernel, c=c, n_valid=n_valid),
        out_shape=[jax.ShapeDtypeStruct((nb, t, GROUP_WIDTH), F32),
                   jax.ShapeDtypeStruct((nb, GROUP_HEADS, HEAD_DIM, HEAD_DIM), F32),
                   jax.ShapeDtypeStruct((nb, 8, GDN_CONV_DIM), F32)],
        grid=(nb, t // c),
        in_specs=[pl.BlockSpec((1, c, GDN_CONV_DIM), lambda b, i: (b, i, 0)),
                  pl.BlockSpec((1, c, SMALL_W), lambda b, i: (b, i, 0)),
                  pl.BlockSpec((1, c, GROUP_WIDTH), lambda b, i: (b, i, 0)),
                  pl.BlockSpec((1, 8, GDN_CONV_DIM), lambda b, i: (b, 0, 0)),
                  pl.BlockSpec((1, GROUP_HEADS, HEAD_DIM, HEAD_DIM), lambda b, i: (b, 0, 0, 0)),
                  pl.BlockSpec((CONV_WIDTH, GDN_CONV_DIM), lambda b, i: (0, 0)),
                  pl.BlockSpec((1, GROUP_WIDTH), lambda b, i: (0, 0))],
        out_specs=[pl.BlockSpec((1, c, GROUP_WIDTH), lambda b, i: (b, i, 0)),
                   pl.BlockSpec((1, GROUP_HEADS, HEAD_DIM, HEAD_DIM), lambda b, i: (b, 0, 0, 0)),
                   pl.BlockSpec((1, 8, GDN_CONV_DIM), lambda b, i: (b, 0, 0))],
        scratch_shapes=[pltpu.VMEM((8, GDN_CONV_DIM), F32),
                        pltpu.VMEM((GROUP_HEADS, HEAD_DIM, HEAD_DIM), F32)],
        compiler_params=_cparams(("arbitrary", "arbitrary")),
        name="gated_deltanet",
    )(gqkv, small, ggate, conv_buf8, s0, conv_w, hg)


def _gmlp_kernel(uv_ref, vg_ref, w_ref, bias_ref, hg_ref, o_ref, v_ref):
    x = uv_ref[0]
    gelu = x * (0.5 * (1.0 + jnp.tanh(math.sqrt(2.0 / math.pi) * (x + 0.044715 * (x * x * x)))))
    u = gelu[:, :GROUP_WIDTH]
    v = _rms(gelu[:, GROUP_WIDTH:], vg_ref[...])
    v_ref[0] = v
    vb = v.astype(BF16)
    n = x.shape[0]
    tri = _iota((n, n), 0) >= _iota((n, n), 1)
    lane_group = _iota((n, GROUP_WIDTH), 1) // HEAD_DIM
    mixed = bias_ref[...]
    for g in range(GROUP_HEADS):
        wg = jnp.where(tri, w_ref[g], 0.0).astype(BF16)
        mixed = mixed + jnp.where(lane_group == g, _dot(wg, vb), 0.0)
    o_ref[0] = _head_norm_lanes(u * mixed, hg_ref[...])


def _gmlp(guv, vg, w_s, bias_full, hg):
    nb, t, _ = guv.shape
    n = w_s.shape[-1]
    return pl.pallas_call(
        _gmlp_kernel,
        out_shape=[jax.ShapeDtypeStruct((nb, t, GROUP_WIDTH), F32),
                   jax.ShapeDtypeStruct((nb, t, GROUP_WIDTH), F32)],
        grid=(nb, t // n),
        in_specs=[pl.BlockSpec((1, n, 2 * GROUP_WIDTH), lambda b, i: (b, i, 0)),
                  pl.BlockSpec((1, GROUP_WIDTH), lambda b, i: (0, 0)),
                  pl.BlockSpec((GROUP_HEADS, n, n), lambda b, i: (0, 0, 0)),
                  pl.BlockSpec((n, GROUP_WIDTH), lambda b, i: (0, 0)),
                  pl.BlockSpec((1, GROUP_WIDTH), lambda b, i: (0, 0))],
        out_specs=[pl.BlockSpec((1, n, GROUP_WIDTH), lambda b, i: (b, i, 0)),
                   pl.BlockSpec((1, n, GROUP_WIDTH), lambda b, i: (b, i, 0))],
        compiler_params=_cparams(("arbitrary", "arbitrary")),
        name="gmlp_gating",
    )(guv, vg, w_s, bias_full, hg)


def _post_kernel(of_ref, os_ref, og_ref, om_ref, x_ref, g1_ref, sh2_ref, sc2_ref, g2_ref, ng_ref,
                 wo_ref, w1_ref, w2_ref, y_ref, x1_ref, h_ref, acc_ref):
    f = pl.program_id(2)

    @pl.when(f == 0)
    def _():
        m = _dot(of_ref[0].astype(BF16), wo_ref[0 * GROUP_WIDTH:1 * GROUP_WIDTH, :])
        m = m + _dot(os_ref[0].astype(BF16), wo_ref[1 * GROUP_WIDTH:2 * GROUP_WIDTH, :])
        m = m + _dot(og_ref[0].astype(BF16), wo_ref[2 * GROUP_WIDTH:3 * GROUP_WIDTH, :])
        m = m + _dot(om_ref[0].astype(BF16), wo_ref[3 * GROUP_WIDTH:4 * GROUP_WIDTH, :])
        x1 = x_ref[0] + g1_ref[0] * _rms(m, ng_ref[1:2, :])
        x1_ref[...] = x1
        h = _rms(x1, ng_ref[2:3, :]) * (1.0 + sc2_ref[0]) + sh2_ref[0]
        h_ref[...] = h.astype(BF16)
        acc_ref[...] = jnp.zeros_like(acc_ref)

    a = jnp.maximum(_dot(h_ref[...], w1_ref[...]), 0.0)
    acc_ref[...] += _dot((a * a).astype(BF16), w2_ref[...])

    @pl.when(f == pl.num_programs(2) - 1)
    def _():
        y_ref[0] = x1_ref[...] + g2_ref[0] * _rms(acc_ref[...], ng_ref[3:4, :])


def _post(o_fox, o_sb, o_gdn, o_gmlp, x, gate1, shift2, scale2, gate2, norm_g, w_out, w1, w2):
    nb, t, d = x.shape
    dff = w1.shape[1]
    tm = min(512, t)
    tf = min(512, dff)
    tmod = gate1.shape[1]
    bm = tm if tmod == t else 1
    mod_spec = pl.BlockSpec((1, bm, d), (lambda b, i, f: (b, i, 0)) if tmod == t else (lambda b, i, f: (b, 0, 0)))
    o_spec = pl.BlockSpec((1, tm, GROUP_WIDTH), lambda b, i, f: (b, i, 0))
    return pl.pallas_call(
        _post_kernel,
        out_shape=jax.ShapeDtypeStruct((nb, t, d), F32),
        grid=(nb, t // tm, dff // tf),
        in_specs=[o_spec, o_spec, o_spec, o_spec,
                  pl.BlockSpec((1, tm, d), lambda b, i, f: (b, i, 0)),
                  mod_spec, mod_spec, mod_spec, mod_spec,
                  pl.BlockSpec((4, d), lambda b, i, f: (0, 0)),
                  pl.BlockSpec((d, d), lambda b, i, f: (0, 0)),
                  pl.BlockSpec((d, tf), lambda b, i, f: (0, f)),
                  pl.BlockSpec((tf, d), lambda b, i, f: (f, 0))],
        out_specs=pl.BlockSpec((1, tm, d), lambda b, i, f: (b, i, 0)),
        scratch_shapes=[pltpu.VMEM((tm, d), F32), pltpu.VMEM((tm, d), BF16), pltpu.VMEM((tm, d), F32)],
        compiler_params=_cparams(("arbitrary", "arbitrary", "arbitrary")),
        name="outproj_mlp",
    )(o_fox, o_sb, o_gdn, o_gmlp, x, gate1, shift2, scale2, gate2, norm_g, w_out, w1, w2)


def _bias_rows(n_pages):
    return GROUP_HEADS * (n_pages + 2)


def _dec_bias_kernel(pt_ref, cache_ref, new_ref, later_ref, o_ref, buf_ref, sem, *, layer, n_pages):
    nb = o_ref.shape[0]
    rows = _bias_rows(n_pages)
    total = nb * n_pages

    def page_copy(i):
        b = i // n_pages
        p = i - b * n_pages
        dst = pl.multiple_of(b * rows + GROUP_HEADS * p, GROUP_HEADS)
        return pltpu.make_async_copy(cache_ref.at[layer, pt_ref[b, p]],
                                     buf_ref.at[pl.ds(dst, GROUP_HEADS), :], sem)

    def start(i, c):
        page_copy(i).start()
        return c

    def wait(i, c):
        page_copy(i).wait()
        return c

    lax.fori_loop(0, total, start, 0)
    lax.fori_loop(0, total, wait, 0)

    after = _tri01(PAGE_SIZE, strict=True, lower=True)
    later = later_ref[...]
    for b in range(nb):
        buf_ref[b * rows + GROUP_HEADS * n_pages:(b + 1) * rows, :] = new_ref[b]
        x = buf_ref[b * rows:(b + 1) * rows, :]
        within = _dot_exact_r(_dot, x, after)
        tot = jnp.broadcast_to(jnp.sum(x, axis=1, keepdims=True), x.shape)
        o_ref[b] = within + _dot_exact_l(_dot, later, tot)


def _dec_bias(page_table, cache_logf_t, new_rows, later, *, layer):
    nb, n_pages = page_table.shape
    rows = _bias_rows(n_pages)
    return pl.pallas_call(
        functools.partial(_dec_bias_kernel, layer=layer, n_pages=n_pages),
        out_shape=jax.ShapeDtypeStruct((nb, rows, PAGE_SIZE), F32),
        grid_spec=pltpu.PrefetchScalarGridSpec(
            num_scalar_prefetch=1,
            grid=(1,),
            in_specs=[pl.BlockSpec(memory_space=pl.ANY),
                      pl.BlockSpec((nb, 2 * GROUP_HEADS, PAGE_SIZE), lambda i, pt: (0, 0, 0)),
                      pl.BlockSpec((rows, rows), lambda i, pt: (0, 0))],
            out_specs=pl.BlockSpec((nb, rows, PAGE_SIZE), lambda i, pt: (0, 0, 0)),
            scratch_shapes=[pltpu.VMEM((nb * rows, PAGE_SIZE), F32), pltpu.SemaphoreType.DMA(())]),
        compiler_params=_cparams(("arbitrary",)),
        name="decode_forget_bias",
    )(page_table, cache_logf_t, new_rows, later)


def _rows_by_head(per_head, rows):
    per = rows // GROUP_HEADS
    rg = _iota((rows, PAGE_SIZE), 0) // per
    out = jnp.broadcast_to(per_head[0:1, :], (rows, PAGE_SIZE))
    for h in range(1, GROUP_HEADS):
        out = jnp.where(rg == h, jnp.broadcast_to(per_head[h:h + 1, :], (rows, PAGE_SIZE)), out)
    return out


def _dec_attn_kernel(pt_ref, *refs, pp, n_pages, n_q):
    del pt_ref
    caches = refs[:4 * pp]
    (qf_ref, qs_ref, nfk_ref, nfv_ref, nsk_ref, nsv_ref, bias_ref, hgf_ref, hgs_ref,
     of_ref, os_ref, qbf_ref, qbs_ref, m_ref, l_ref, accf_ref, run_ref, accs_ref, ft_ref) = refs[4 * pp:]
    g = pl.program_id(1)
    rows = GROUP_HEADS * n_q
    row_head = _iota((rows, GROUP_WIDTH), 0) // n_q
    lane_head = _iota((rows, GROUP_WIDTH), 1) // HEAD_DIM
    own = row_head == lane_head
    later = _tri01(PAGE_SIZE, strict=True, lower=True)

    def fox_page(kf, vf, bias16, mask, transposed):
        qk, pv = (_dot, _dot_nt) if transposed else (_dot_nt, _dot)
        s = qk(qbf_ref[...], kf) + (bias16 - ft_ref[...])
        if mask is not None:
            s = jnp.where(mask, s, NEG_BIG)
        m = m_ref[...]
        m_new = jnp.maximum(m, jnp.max(s, axis=1, keepdims=True))
        alpha = jnp.exp(m - m_new)
        p = jnp.exp(s - m_new)
        l_ref[...] = alpha * l_ref[...] + jnp.sum(p, axis=1, keepdims=True)
        accf_ref[...] = alpha * accf_ref[...] + pv(p.astype(BF16), vf)
        m_ref[...] = m_new

    def sb_page(ks, vs, mask, transposed):
        qk, pv = (_dot, _dot_nt) if transposed else (_dot_nt, _dot)
        z = qk(qbs_ref[...], ks)
        t = _softplus_parts(z)
        log_keep = -(jnp.maximum(z, 0.0) + t)
        if mask is not None:
            log_keep = jnp.where(mask, log_keep, 0.0)
        after = _dot_exact_r(_dot, log_keep, later, pieces=2) + run_ref[...]
        wgt = jnp.exp(log_keep + z + after)
        if mask is not None:
            wgt = jnp.where(mask, wgt, 0.0)
        accs_ref[...] = accs_ref[...] + pv(wgt.astype(BF16), vs)
        run_ref[...] = run_ref[...] + jnp.sum(log_keep, axis=1, keepdims=True)

    @pl.when(g == 0)
    def _():
        qbf_ref[...] = jnp.where(own, qf_ref[0] * QK_SCALE, 0.0).astype(BF16)
        qbs_ref[...] = jnp.where(own, qs_ref[0] * QK_SCALE, 0.0).astype(BF16)
        m_ref[...] = jnp.full_like(m_ref, NEG_BIG)
        l_ref[...] = jnp.zeros_like(l_ref)
        accf_ref[...] = jnp.zeros_like(accf_ref)
        accs_ref[...] = jnp.zeros_like(accs_ref)
        run_ref[...] = jnp.zeros_like(run_ref)
        bias16 = _rows_by_head(bias_ref[0, GROUP_HEADS * n_pages:GROUP_HEADS * (n_pages + 1), :], rows)
        qpos = _iota((rows, PAGE_SIZE), 0) % n_q
        kpos = _iota((rows, PAGE_SIZE), 1)
        ft_ref[...] = jnp.sum(jnp.where(kpos == qpos, bias16, 0.0), axis=1, keepdims=True)
        pad = jnp.zeros((PAGE_SIZE - 8, GROUP_WIDTH), F32)

        def page_of(r):
            return jnp.concatenate([r[0], pad], axis=0).astype(BF16)

        fox_page(page_of(nfk_ref), page_of(nfv_ref), bias16, (kpos <= qpos) & (kpos < n_q), False)
        sb_page(page_of(nsk_ref), page_of(nsv_ref), kpos < qpos, False)

    @pl.when(g > 0)
    def _():
        first = pl.multiple_of(GROUP_HEADS * (n_pages - g * pp), GROUP_HEADS * pp)
        slab = bias_ref[0, pl.ds(first, GROUP_HEADS * pp), :]
        for r in range(pp):
            lo = GROUP_HEADS * (pp - 1 - r)
            bias16 = _rows_by_head(slab[lo:lo + GROUP_HEADS, :], rows)
            kf, vf, ks, vs = (caches[4 * r + i][0, 0].astype(BF16) for i in range(4))
            fox_page(kf, vf, bias16, None, True)
            sb_page(ks, vs, None, True)

    @pl.when(g == pl.num_programs(1) - 1)
    def _():
        def fold(acc):
            a = jnp.where(own, acc, 0.0)
            out = a[0:n_q, :]
            for h in range(1, GROUP_HEADS):
                out = out + a[h * n_q:(h + 1) * n_q, :]
            return out

        of_ref[0] = _head_norm_lanes(fold(accf_ref[...] / l_ref[...]), hgf_ref[...])
        os_ref[0] = _head_norm_lanes(fold(accs_ref[...]), hgs_ref[...])


def _dec_attention(page_table, cfk, cfv, csk, csv, qf16, qs16, nfk, nfv, nsk, nsv, bias, hgf, hgs, *, layer, n_q):
    nb, n_pages = page_table.shape
    pp = 4
    assert n_pages % pp == 0
    n_groups = n_pages // pp
    rows = GROUP_HEADS * n_q

    def cache_spec(r):
        def imap(b, g, pt):
            page = n_pages - 1 - (jnp.maximum(g - 1, 0) * pp + r)
            return (layer, pt[b, page], 0, 0)
        return pl.BlockSpec((1, 1, GROUP_WIDTH, PAGE_SIZE), imap)

    cache_specs, cache_args = [], []
    for r in range(pp):
        for arr in (cfk, cfv, csk, csv):
            cache_specs.append(cache_spec(r))
            cache_args.append(arr)
    per_b3 = lambda b, g, pt: (b, 0, 0)
    const2 = lambda b, g, pt: (0, 0)
    return pl.pallas_call(
        functools.partial(_dec_attn_kernel, pp=pp, n_pages=n_pages, n_q=n_q),
        out_shape=[jax.ShapeDtypeStruct((nb, n_q, GROUP_WIDTH), F32)] * 2,
        grid_spec=pltpu.PrefetchScalarGridSpec(
            num_scalar_prefetch=1,
            grid=(nb, n_groups + 1),
            in_specs=cache_specs + [
                pl.BlockSpec((1, rows, GROUP_WIDTH), per_b3),
                pl.BlockSpec((1, rows, GROUP_WIDTH), per_b3),
                pl.BlockSpec((1, 8, GROUP_WIDTH), per_b3),
                pl.BlockSpec((1, 8, GROUP_WIDTH), per_b3),
                pl.BlockSpec((1, 8, GROUP_WIDTH), per_b3),
                pl.BlockSpec((1, 8, GROUP_WIDTH), per_b3),
                pl.BlockSpec((1, _bias_rows(n_pages), PAGE_SIZE), per_b3),
                pl.BlockSpec((1, GROUP_WIDTH), const2),
                pl.BlockSpec((1, GROUP_WIDTH), const2)],
            out_specs=[pl.BlockSpec((1, n_q, GROUP_WIDTH), per_b3)] * 2,
            scratch_shapes=[pltpu.VMEM((rows, GROUP_WIDTH), BF16), pltpu.VMEM((rows, GROUP_WIDTH), BF16),
                            pltpu.VMEM((rows, 1), F32), pltpu.VMEM((rows, 1), F32),
                            pltpu.VMEM((rows, GROUP_WIDTH), F32),
                            pltpu.VMEM((rows, 1), F32), pltpu.VMEM((rows, GROUP_WIDTH), F32),
                            pltpu.VMEM((rows, 1), F32)]),
        compiler_params=_cparams(("arbitrary", "arbitrary")),
        name="decode_attention",
    )(page_table, *cache_args, qf16, qs16, nfk, nfv, nsk, nsv, bias, hgf, hgs)


def _permute_w_in(w_in):
    gw, gh = GROUP_WIDTH, GROUP_HEADS
    o = 0
    fq = w_in[:, o:o + gw]; o += gw
    fk = w_in[:, o:o + gw]; o += gw
    fv = w_in[:, o:o + gw]; o += gw
    ff = w_in[:, o:o + gh]; o += gh
    sq = w_in[:, o:o + gw]; o += gw
    sk = w_in[:, o:o + gw]; o += gw
    sv = w_in[:, o:o + gw]; o += gw
    gqkv = w_in[:, o:o + GDN_CONV_DIM]; o += GDN_CONV_DIM
    ga = w_in[:, o:o + gh]; o += gh
    gb = w_in[:, o:o + gh]; o += gh
    ggate = w_in[:, o:o + gw]; o += gw
    guv = w_in[:, o:o + 2 * gw]
    pad = jnp.zeros((w_in.shape[0], SMALL_W - 4 * gh), w_in.dtype)
    return jnp.concatenate([fq, fk, fv, sq, sk, sv, gqkv, ggate, guv, ff, ga, gb, ff, pad], axis=1).astype(BF16)


def _small_params(b_forget, a_log, dt_bias):
    z4 = jnp.zeros((GROUP_HEADS,), F32)
    zpad = jnp.zeros((SMALL_W - 4 * GROUP_HEADS,), F32)
    bias = jnp.concatenate([b_forget, dt_bias, z4, b_forget, zpad])
    alog = jnp.concatenate([z4, a_log, z4, z4, zpad])
    return jnp.zeros((8, SMALL_W), F32).at[0].set(bias).at[1].set(alog)


def _later_pages_matrix(n_pages):
    idx = jnp.arange(_bias_rows(n_pages))
    page, head = idx // GROUP_HEADS, idx % GROUP_HEADS
    return ((head[:, None] == head[None, :]) & (page[None, :] > page[:, None])).astype(BF16)


def _pages_transposed(cache):
    depth, n_phys = cache.shape[:2]
    return jnp.transpose(cache, (0, 1, 3, 4, 2)).reshape(depth, n_phys, GROUP_WIDTH, PAGE_SIZE)


def _heads_last(a_t, t):
    return jnp.transpose(a_t.reshape(a_t.shape[0], GROUP_HEADS, HEAD_DIM, t), (0, 3, 1, 2))


def kernel(x_prompt, x_sample, cache_fox_k, cache_fox_v, cache_fox_logf, cache_sb_k, cache_sb_v, state_gdn, state_gdn_conv, page_table, c_prompt, c_sample, ada_w, ada_b, norm_g, w_in, b_forget, gdn_conv_w, gdn_a_log, gdn_dt_bias, gmlp_v_norm_g, gmlp_w_s, gmlp_b_s, head_norm_g, w_out, w_ff1, w_ff2):
    depth = ada_w.shape[0]
    n_p, t_p, d = x_prompt.shape
    n_s, t_s, _ = x_sample.shape
    n_phys = cache_fox_k.shape[1]
    rows_s = n_s * t_s
    gw, gh = GROUP_WIDTH, GROUP_HEADS

    mods = _modulation(jnp.concatenate([c_prompt, c_sample], axis=0), ada_w, ada_b)
    mods_p = mods[:, :n_p].reshape(depth, n_p, 1, N_MOD, d)
    mods_s = jnp.repeat(mods[:, n_p:], t_s, axis=1).reshape(depth, 1, rows_s, N_MOD, d)

    n_pages = page_table.shape[1]
    cfk, cfv = _pages_transposed(cache_fox_k), _pages_transposed(cache_fox_v)
    csk, csv = _pages_transposed(cache_sb_k), _pages_transposed(cache_sb_v)
    clogf = jnp.swapaxes(cache_fox_logf, 2, 3)
    later = _later_pages_matrix(n_pages)

    zeros_cb = jnp.zeros((n_p, 8, GDN_CONV_DIM), F32)
    zeros_s0 = jnp.zeros((n_p, gh, HEAD_DIM, HEAD_DIM), F32)
    eye_s = jnp.eye(n_s, dtype=F32)

    x_p = x_prompt
    x_s = x_sample.reshape(1, rows_s, d)
    outs_p, outs_s = [], []
    for l in range(depth):
        w_l = _permute_w_in(w_in[l])
        wt_l = jnp.concatenate([w_l[:, C_FK:C_FK + 2 * gw], w_l[:, C_SK:C_SK + 2 * gw]], axis=1).T
        sp = _small_params(b_forget[l], gdn_a_log[l], gdn_dt_bias[l])
        hg = head_norm_g[l].reshape(4, 1, gw)
        wo, w1, w2 = w_out[l].astype(BF16), w_ff1[l].astype(BF16), w_ff2[l].astype(BF16)
        ng = norm_g[l]
        bias_p = jnp.repeat(gmlp_b_s[l].T, HEAD_DIM, axis=1)
        vg = gmlp_v_norm_g[l].reshape(1, gw)

        mp = [mods_p[l, :, :, i] for i in range(N_MOD)]
        fq, fk, fv, sq, sk, sv, gqkv, ggate, guv, small = _in_projection(
            x_p, mp[0], mp[1], ng[0:1], w_l, wt_l, sp, cumsum=True, kv_transposed=True)
        frow = jnp.swapaxes(small[:, :, SM_CUM:SM_CUM + gh], 1, 2)
        o_fox = _fox_prompt(fq, fk, fv, small, frow, hg[0])
        o_sb = _sb_prompt(sq, sk, sv, hg[1])
        o_gdn, s_new, cb_new = _gated_deltanet(gqkv, small, ggate, zeros_cb, zeros_s0, gdn_conv_w[l], hg[2],
                                               c=GDN_CHUNK, n_valid=None)
        o_gmlp, _ = _gmlp(guv, vg, gmlp_w_s[l], bias_p, hg[3])
        x_p = _post(o_fox, o_sb, o_gdn, o_gmlp, x_p, mp[2], mp[3], mp[4], mp[5], ng, wo, w1, w2)
        outs_p.append((_heads_last(fk, t_p), _heads_last(fv, t_p), small[:, :, SM_LOGF:SM_LOGF + gh],
                       _heads_last(sk, t_p), _heads_last(sv, t_p), s_new, cb_new[:, 8 - (CONV_WIDTH - 1):]))

        ms = [mods_s[l, :, :, i] for i in range(N_MOD)]
        fq, fk, fv, sq, sk, sv, gqkv, ggate, guv, small = _in_projection(
            x_s, ms[0], ms[1], ng[0:1], w_l, wt_l, sp, cumsum=False, kv_transposed=False)
        logf_new = small[0, :, SM_LOGF:SM_LOGF + gh]
        new_rows = jnp.pad(jnp.swapaxes(logf_new.reshape(n_s, t_s, gh), 1, 2),
                           ((0, 0), (0, gh), (0, PAGE_SIZE - t_s)))
        bias = _dec_bias(page_table, clogf, new_rows, later, layer=l)
        per_seq = lambda a: a.reshape(n_s, t_s, gw)
        pad8 = lambda a: jnp.pad(per_seq(a), ((0, 0), (0, 8 - t_s), (0, 0)))
        tile_q = lambda a: jnp.tile(per_seq(a), (1, gh, 1))
        o_fox, o_sb = _dec_attention(page_table, cfk, cfv, csk, csv, tile_q(fq), tile_q(sq),
                                     pad8(fk), pad8(fv), pad8(sk), pad8(sv), bias, hg[0], hg[1],
                                     layer=l, n_q=t_s)
        cs = GDN_CHUNK_SMALL
        padc = lambda a: jnp.pad(a.reshape(n_s, t_s, a.shape[-1]), ((0, 0), (0, cs - t_s), (0, 0)))
        cb8 = jnp.pad(state_gdn_conv[l], ((0, 0), (8 - (CONV_WIDTH - 1), 0), (0, 0)))
        o_gdn, s_new, cb_new = _gated_deltanet(padc(gqkv), padc(small), padc(ggate), cb8, state_gdn[l],
                                               gdn_conv_w[l], hg[2], c=cs, n_valid=t_s)
        o_gdn = o_gdn[:, :t_s].reshape(1, rows_s, gw)
        w_blk = jnp.einsum('ab,gij->gaibj', eye_s, gmlp_w_s[l][:, :t_s, :t_s]).reshape(gh, rows_s, rows_s)
        bias_s = jnp.tile(bias_p[:t_s], (n_s, 1))
        o_gmlp, v_rows = _gmlp(guv, vg, w_blk, bias_s, hg[3])
        x_s = _post(o_fox.reshape(1, rows_s, gw), o_sb.reshape(1, rows_s, gw), o_gdn, o_gmlp, x_s,
                    ms[2], ms[3], ms[4], ms[5], ng, wo, w1, w2)
        outs_s.append((per_seq(fk[0]), per_seq(fv[0]), logf_new.reshape(n_s, t_s, gh), per_seq(sk[0]),
                       per_seq(sv[0]), s_new, cb_new[:, 8 - (CONV_WIDTH - 1):], per_seq(v_rows[0])))

    def stacked(states, i, shape=None):
        a = jnp.stack([s[i] for s in states])
        return a if shape is None else a.reshape(shape)

    hp = (depth, n_p, t_p, gh, HEAD_DIM)
    hs = (depth, n_s, t_s, gh, HEAD_DIM)
    return (x_p, x_s.reshape(n_s, t_s, d),
            stacked(outs_p, 0, hp), stacked(outs_p, 1, hp), stacked(outs_p, 2), stacked(outs_p, 3, hp),
            stacked(outs_p, 4, hp), stacked(outs_p, 5), stacked(outs_p, 6),
            stacked(outs_s, 0, hs), stacked(outs_s, 1, hs), stacked(outs_s, 2), stacked(outs_s, 3, hs),
            stacked(outs_s, 4, hs), stacked(outs_s, 5), stacked(outs_s, 6), stacked(outs_s, 7))
```

```python
import functools
import math

import jax
import jax.numpy as jnp
from jax import lax
from jax.experimental import pallas as pl
from jax.experimental.pallas import tpu as pltpu

F32 = jnp.float32
BF16 = jnp.bfloat16

HEAD_DIM = 64
GROUP_HEADS = 4
GROUP_WIDTH = GROUP_HEADS * HEAD_DIM
CONV_WIDTH = 4
GDN_CONV_DIM = 3 * GROUP_WIDTH
GDN_CHUNK = 64
GDN_CHUNK_SMALL = 16
MLP_CHUNK = 128
PAGE_SIZE = 128
N_MOD = 6
EPS = 1e-6
QK_SCALE = HEAD_DIM ** -0.5
NEG_BIG = -1e30

SM_LOGF = 0
SM_G = 4
SM_BETA = 8
SM_CUM = 12
SMALL_W = 128

C_FQ, C_FK, C_FV, C_SQ, C_SK, C_SV = (i * GROUP_WIDTH for i in range(6))
C_GQKV = 6 * GROUP_WIDTH
C_GGATE = C_GQKV + GDN_CONV_DIM
C_GUV = C_GGATE + GROUP_WIDTH
C_SMALL = C_GUV + 2 * GROUP_WIDTH
IN_PAD_WIDTH = C_SMALL + SMALL_W

VMEM_LIMIT = 56 * 1024 * 1024

ATTN_Q_TILE = 512
SB_KEY_TILE = 256
GDN_BATCH = 4
DEC_PAGES = 8
MLP_ROW_TILE = 1024


def _cparams(sem):
    return pltpu.CompilerParams(dimension_semantics=sem, vmem_limit_bytes=VMEM_LIMIT)


def _dot(a, b):
    return jnp.dot(a, b, preferred_element_type=F32)


def _dot_nt(a, b):
    return lax.dot_general(a, b, (((1,), (1,)), ((), ())), preferred_element_type=F32)


def _split2(a):
    hi = a.astype(BF16)
    lo = (a - hi.astype(F32)).astype(BF16)
    return hi, lo


def _split3(a):
    hi = a.astype(BF16)
    r = a - hi.astype(F32)
    mid = r.astype(BF16)
    lo = (r - mid.astype(F32)).astype(BF16)
    return hi, mid, lo


def _dot3(dot, a, b):
    ah, al = _split2(a)
    bh, bl = _split2(b)
    return dot(ah, bh) + (dot(ah, bl) + dot(al, bh))


def _dot_exact_l(dot, a01, b, pieces=3):
    parts = _split3(b) if pieces == 3 else _split2(b)
    out = dot(a01, parts[0])
    for p in parts[1:]:
        out = out + dot(a01, p)
    return out


def _dot_exact_r(dot, a, b01, pieces=3):
    parts = _split3(a) if pieces == 3 else _split2(a)
    out = dot(parts[0], b01)
    for p in parts[1:]:
        out = out + dot(p, b01)
    return out


def _softplus_parts(x):
    return jnp.log1p(jnp.exp(-jnp.abs(x)))


def _sigmoid(x):
    return 1.0 / (1.0 + jnp.exp(-x))


def _silu(x):
    return x * _sigmoid(x)


def _rms(x, g):
    return x * lax.rsqrt(jnp.mean(x * x, axis=-1, keepdims=True) + EPS) * g


def _iota(shape, dim):
    return lax.broadcasted_iota(jnp.int32, shape, dim)


def _tri01(n, strict, lower, dtype=BF16):
    r, c = _iota((n, n), 0), _iota((n, n), 1)
    if lower:
        m = (r > c) if strict else (r >= c)
    else:
        m = (r < c) if strict else (r <= c)
    return jnp.where(m, 1.0, 0.0).astype(dtype)


def _group_ones(n, group):
    r, c = _iota((n, n), 0), _iota((n, n), 1)
    return jnp.where((r // group) == (c // group), 1.0, 0.0).astype(BF16)


def _head_norm_lanes(x, gain):
    ones = _group_ones(x.shape[-1], HEAD_DIM)
    ms = _dot_exact_r(_dot, x * x, ones) * (1.0 / HEAD_DIM)
    return x * lax.rsqrt(ms + EPS) * gain


def _mod_kernel(c_ref, w_ref, b_ref, o_ref):
    a = _silu(c_ref[...]).astype(BF16)
    o_ref[0] = _dot(a, w_ref[0].astype(BF16)) + b_ref[0]


def _modulation(c_all, ada_w, ada_b):
    depth, d, n = ada_w.shape
    rows = c_all.shape[0]
    tn = 1536 if n % 1536 == 0 else n
    return pl.pallas_call(
        _mod_kernel,
        out_shape=jax.ShapeDtypeStruct((depth, rows, n), F32),
        grid=(depth, n // tn),
        in_specs=[pl.BlockSpec((rows, d), lambda l, j: (0, 0)),
                  pl.BlockSpec((1, d, tn), lambda l, j: (l, 0, j)),
                  pl.BlockSpec((1, 1, tn), lambda l, j: (l, 0, j))],
        out_specs=pl.BlockSpec((1, rows, tn), lambda l, j: (l, 0, j)),
        compiler_params=_cparams(("arbitrary", "arbitrary")),
        name="adaln_mod",
    )(c_all, ada_w, ada_b.reshape(depth, 1, n))


def _inproj_kernel(x_ref, sh_ref, sc_ref, g_ref, w_ref, wt_ref, sp_ref,
                   fq, fk, fv, sq, sk, sv, gqkv, ggate, guv, small, carry_ref, *, cumsum, kv_transposed):
    x = x_ref[0]
    h = _rms(x, g_ref[...]) * (1.0 + sc_ref[0]) + sh_ref[0]
    hb = h.astype(BF16)

    def seg(off, n):
        return _dot(hb, w_ref[:, off:off + n])

    def seg_t(i):
        return _dot_nt(wt_ref[i * GROUP_WIDTH:(i + 1) * GROUP_WIDTH, :], hb)

    fq[0] = seg(C_FQ, GROUP_WIDTH)
    sq[0] = seg(C_SQ, GROUP_WIDTH)
    if kv_transposed:
        fk[0] = seg_t(0)
        fv[0] = seg_t(1)
        sk[0] = seg_t(2)
        sv[0] = seg_t(3)
    else:
        fk[0] = seg(C_FK, GROUP_WIDTH)
        fv[0] = seg(C_FV, GROUP_WIDTH)
        sk[0] = seg(C_SK, GROUP_WIDTH)
        sv[0] = seg(C_SV, GROUP_WIDTH)
    gqkv[0] = seg(C_GQKV, GDN_CONV_DIM)
    ggate[0] = seg(C_GGATE, GROUP_WIDTH)
    guv[0] = seg(C_GUV, 2 * GROUP_WIDTH)

    z = seg(C_SMALL, SMALL_W)
    tm = z.shape[0]
    lane = _iota(z.shape, 1)
    zz = z + sp_ref[0:1, :]
    t = _softplus_parts(zz)
    logf = jnp.minimum(zz, 0.0) - t
    gdec = -jnp.exp(sp_ref[1:2, :]) * (jnp.maximum(zz, 0.0) + t)
    beta = _sigmoid(z)
    out = jnp.where(lane < SM_G, logf, jnp.where(lane < SM_BETA, gdec, jnp.where(lane < SM_CUM, beta, 0.0)))
    if cumsum:
        @pl.when(pl.program_id(1) == 0)
        def _():
            carry_ref[...] = jnp.zeros_like(carry_ref)

        lf = jnp.where((lane >= SM_CUM) & (lane < SM_CUM + GROUP_HEADS), logf, 0.0)
        cum = _dot_exact_l(_dot, _tri01(tm, strict=False, lower=True), lf) + carry_ref[0:1, :]
        carry_ref[0:1, :] = cum[tm - 1:tm, :]
        out = out + cum
    small[0] = out


def _in_projection(x, shift, scale, g, w, wt, sp, *, cumsum, kv_transposed):
    nb, t, d = x.shape
    tm = min(512, t)
    tmod = shift.shape[1]
    bm = tm if tmod == t else 1
    mod_spec = pl.BlockSpec((1, bm, d), (lambda b, i: (b, i, 0)) if tmod == t else (lambda b, i: (b, 0, 0)))
    gw = GROUP_WIDTH
    row = lambda wd: (jax.ShapeDtypeStruct((nb, t, wd), F32), pl.BlockSpec((1, tm, wd), lambda b, i: (b, i, 0)))
    col = (jax.ShapeDtypeStruct((nb, gw, t), F32), pl.BlockSpec((1, gw, tm), lambda b, i: (b, 0, i)))
    kv = col if kv_transposed else row(gw)
    outs = [row(gw), kv, kv, row(gw), kv, kv, row(GDN_CONV_DIM), row(gw), row(2 * gw), row(SMALL_W)]
    return pl.pallas_call(
        functools.partial(_inproj_kernel, cumsum=cumsum, kv_transposed=kv_transposed),
        out_shape=[o[0] for o in outs],
        grid=(nb, t // tm),
        in_specs=[pl.BlockSpec((1, tm, d), lambda b, i: (b, i, 0)),
                  mod_spec, mod_spec,
                  pl.BlockSpec((1, d), lambda b, i: (0, 0)),
                  pl.BlockSpec((d, IN_PAD_WIDTH), lambda b, i: (0, 0)),
                  pl.BlockSpec((4 * gw, d), lambda b, i: (0, 0)),
                  pl.BlockSpec((8, SMALL_W), lambda b, i: (0, 0))],
        out_specs=[o[1] for o in outs],
        scratch_shapes=[pltpu.VMEM((8, SMALL_W), F32)],
        compiler_params=_cparams(("arbitrary", "arbitrary")),
        name="in_projection",
    )(x, shift, scale, g, w, wt, sp)


def _fox_kernel(q_ref, kt_ref, vt_ref, smq_ref, frow_ref, hg_ref, o_ref, ks_ref, vs_ref, *, tq):
    qi = pl.program_id(1)

    @pl.when(qi == 0)
    def _():
        ks_ref[...] = kt_ref[0].astype(BF16)
        vs_ref[...] = vt_ref[0].astype(BF16)

    q_all = q_ref[0]
    sm = smq_ref[0]
    qhs = [(q_all[:, h * HEAD_DIM:(h + 1) * HEAD_DIM] * QK_SCALE).astype(BF16) for h in range(GROUP_HEADS)]
    f_qs = [sm[:, SM_CUM + h:SM_CUM + h + 1] for h in range(GROUP_HEADS)]

    def step(j, carry, masked):
        start = pl.multiple_of(j * tq, tq)
        heads = range(GROUP_HEADS)
        rows = [slice(h * HEAD_DIM, (h + 1) * HEAD_DIM) for h in heads]
        qk = [_dot(qhs[h], ks_ref[rows[h], pl.ds(start, tq)]) for h in heads]
        stats, ps = [], []
        for h in heads:
            m, l, _ = carry[h]
            f_k = frow_ref[0, h:h + 1, pl.ds(start, tq)]
            s = qk[h] + (f_qs[h] - f_k)
            if masked:
                s = jnp.where(_iota((tq, tq), 1) <= _iota((tq, tq), 0), s, NEG_BIG)
            m_new = jnp.maximum(m, jnp.max(s, axis=1, keepdims=True))
            alpha = jnp.exp(m - m_new)
            p = jnp.exp(s - m_new)
            stats.append((m_new, alpha, alpha * l + jnp.sum(p, axis=1, keepdims=True)))
            ps.append(p.astype(BF16))
        pv = [_dot_nt(ps[h], vs_ref[rows[h], pl.ds(start, tq)]) for h in heads]
        return tuple((stats[h][0], stats[h][2], stats[h][1] * carry[h][2] + pv[h]) for h in heads)

    init = tuple((jnp.full((tq, 1), NEG_BIG, F32), jnp.zeros((tq, 1), F32), jnp.zeros((tq, HEAD_DIM), F32))
                 for _ in range(GROUP_HEADS))
    carry = lax.fori_loop(0, qi, functools.partial(step, masked=False), init)
    carry = step(qi, carry, True)
    outs = []
    for h in range(GROUP_HEADS):
        m, l, acc = carry[h]
        outs.append(_rms(acc / l, hg_ref[:, h * HEAD_DIM:(h + 1) * HEAD_DIM]))
    o_ref[0] = jnp.concatenate(outs, axis=1)


def _fox_prompt(q, kt, vt, small, frow, hg):
    nb, t, w = q.shape
    tq = min(ATTN_Q_TILE, t)
    return pl.pallas_call(
        functools.partial(_fox_kernel, tq=tq),
        out_shape=jax.ShapeDtypeStruct((nb, t, w), F32),
        grid=(nb, t // tq),
        in_specs=[pl.BlockSpec((1, tq, w), lambda b, i: (b, i, 0)),
                  pl.BlockSpec((1, w, t), lambda b, i: (b, 0, 0)),
                  pl.BlockSpec((1, w, t), lambda b, i: (b, 0, 0)),
                  pl.BlockSpec((1, tq, SMALL_W), lambda b, i: (b, i, 0)),
                  pl.BlockSpec((1, GROUP_HEADS, t), lambda b, i: (b, 0, 0)),
                  pl.BlockSpec((1, w), lambda b, i: (0, 0))],
        out_specs=pl.BlockSpec((1, tq, w), lambda b, i: (b, i, 0)),
        scratch_shapes=[pltpu.VMEM((w, t), BF16), pltpu.VMEM((w, t), BF16)],
        compiler_params=_cparams(("arbitrary", "arbitrary")),
        name="fox_prompt",
    )(q, kt, vt, small, frow, hg)


def _sb_kernel(q_ref, kt_ref, vt_ref, hg_ref, o_ref, ks_ref, vs_ref, *, tq, tk):
    qi = pl.program_id(1)

    @pl.when(qi == 0)
    def _():
        ks_ref[...] = kt_ref[0].astype(BF16)
        vs_ref[...] = vt_ref[0].astype(BF16)

    q_all = q_ref[0]
    qhs = [(q_all[:, h * HEAD_DIM:(h + 1) * HEAD_DIM] * QK_SCALE).astype(BF16) for h in range(GROUP_HEADS)]
    later = _tri01(tk, strict=True, lower=True)
    per_q = tq // tk

    def step(j, carry, masked):
        start = pl.multiple_of(j * tk, tk)
        if masked:
            before = (_iota((tq, tk), 1) + j * tk) < (_iota((tq, tk), 0) + qi * tq)
        heads = range(GROUP_HEADS)
        rows = [slice(h * HEAD_DIM, (h + 1) * HEAD_DIM) for h in heads]
        zs = [_dot(qhs[h], ks_ref[rows[h], pl.ds(start, tk)]) for h in heads]
        log_keeps = []
        for h in heads:
            z = zs[h]
            log_keep = -(jnp.maximum(z, 0.0) + _softplus_parts(z))
            if masked:
                log_keep = jnp.where(before, log_keep, 0.0)
            log_keeps.append(log_keep)
        afters = [_dot_exact_r(_dot, log_keeps[h], later, pieces=2) for h in heads]
        wgts = []
        for h in heads:
            wgt = jnp.exp(log_keeps[h] + zs[h] + (afters[h] + carry[h][0]))
            if masked:
                wgt = jnp.where(before, wgt, 0.0)
            wgts.append(wgt.astype(BF16))
        pv = [_dot_nt(wgts[h], vs_ref[rows[h], pl.ds(start, tk)]) for h in heads]
        return tuple((carry[h][0] + jnp.sum(log_keeps[h], axis=1, keepdims=True), carry[h][1] + pv[h]) for h in heads)

    carry = tuple((jnp.zeros((tq, 1), F32), jnp.zeros((tq, HEAD_DIM), F32)) for _ in range(GROUP_HEADS))
    for d in range(per_q):
        carry = step(qi * per_q + (per_q - 1 - d), carry, True)
    n_before = qi * per_q
    carry = lax.fori_loop(0, n_before, lambda i, c: step(n_before - 1 - i, c, False), carry)
    outs = [_rms(carry[h][1], hg_ref[:, h * HEAD_DIM:(h + 1) * HEAD_DIM]) for h in range(GROUP_HEADS)]
    o_ref[0] = jnp.concatenate(outs, axis=1)


def _sb_prompt(q, kt, vt, hg):
    nb, t, w = q.shape
    tq = min(ATTN_Q_TILE, t)
    tk = min(SB_KEY_TILE, t)
    return pl.pallas_call(
        functools.partial(_sb_kernel, tq=tq, tk=tk),
        out_shape=jax.ShapeDtypeStruct((nb, t, w), F32),
        grid=(nb, t // tq),
        in_specs=[pl.BlockSpec((1, tq, w), lambda b, i: (b, i, 0)),
                  pl.BlockSpec((1, w, t), lambda b, i: (b, 0, 0)),
                  pl.BlockSpec((1, w, t), lambda b, i: (b, 0, 0)),
                  pl.BlockSpec((1, w), lambda b, i: (0, 0))],
        out_specs=pl.BlockSpec((1, tq, w), lambda b, i: (b, i, 0)),
        scratch_shapes=[pltpu.VMEM((w, t), BF16), pltpu.VMEM((w, t), BF16)],
        compiler_params=_cparams(("arbitrary", "arbitrary")),
        name="sb_prompt",
    )(q, kt, vt, hg)


def _unit_lower_inverse(m_low, c):
    r, cc = _iota((c, c), 0), _iota((c, c), 1)
    eye = jnp.where(r == cc, 1.0, 0.0)
    ps = [eye - m for m in m_low]
    mps = list(m_low)
    for _ in range(int(math.log2(c)) - 1):
        mpb = [mp.astype(BF16) for mp in mps]
        mps = [_dot(b, b) for b in mpb]
        ps = [p + _dot(p.astype(BF16), mp.astype(BF16)) for p, mp in zip(ps, mps)]
    resid = [(eye - p) - _dot3(_dot, m, p) for m, p in zip(m_low, ps)]
    return [p + _dot(p.astype(BF16), rs.astype(BF16)) for p, rs in zip(ps, resid)]


def _gdn_kernel(x_ref, sm_ref, gate_ref, cb_ref, s0_ref, cw_ref, hg_ref,
                o_ref, sn_ref, cbn_ref, tail_ref, st_ref, *, c, n_valid):
    ci = pl.program_id(1)
    n_chunks = pl.num_programs(1)
    n_seq = x_ref.shape[0]

    @pl.when(ci == 0)
    def _():
        tail_ref[...] = cb_ref[...]
        st_ref[...] = s0_ref[...]

    rowi = _iota((c, GDN_CONV_DIM), 0)
    lane = _iota((c, SMALL_W), 1)
    cum_rows = _tri01(c, strict=False, lower=True)
    sel = jnp.where(_iota((16, SMALL_W), 1) == _iota((16, SMALL_W), 0) + SM_G, 1.0, 0.0).astype(BF16)
    eye_d = jnp.where(_iota((HEAD_DIM, HEAD_DIM), 0) == _iota((HEAD_DIM, HEAD_DIM), 1), 1.0, 0.0).astype(BF16)
    ri, cj = _iota((c, c), 0), _iota((c, c), 1)
    incl = ri >= cj
    strict = ri > cj
    n_last = c if n_valid is None else n_valid
    if n_valid is not None:
        valid = _iota((c, 1), 0) < n_valid

    seqs = range(n_seq)
    ys, sms = [], []
    for b in seqs:
        x = x_ref[b]
        tail = tail_ref[b]
        y = x * cw_ref[CONV_WIDTH - 1:CONV_WIDTH, :]
        tail_tiled = jnp.concatenate([tail] * (c // 8), axis=0)
        for s in range(1, CONV_WIDTH):
            shifted = jnp.where(rowi < s, pltpu.roll(tail_tiled, s, axis=0), pltpu.roll(x, s, axis=0))
            y = y + shifted * cw_ref[CONV_WIDTH - 1 - s:CONV_WIDTH - s, :]
        ys.append(_silu(y))

        if n_last % 8 == 0:
            new_tail = x[n_last - 8:n_last, :]
        else:
            cat = jnp.concatenate([tail, x[0:8, :]], axis=0)
            new_tail = pltpu.roll(cat, 16 - n_last, axis=0)[0:8, :]
        tail_ref[b] = x[c - 8:c, :]
        cbn_ref[b] = new_tail

        sm = sm_ref[b]
        if n_valid is not None:
            sm = jnp.where(valid, sm, 0.0)
        sms.append(sm)

    g_ls = [jnp.where((lane >= SM_G) & (lane < SM_G + GROUP_HEADS), sm, 0.0) for sm in sms]
    gcums = [_dot_exact_l(_dot, cum_rows, g_l) for g_l in g_ls]
    grows = [_dot_exact_l(_dot_nt, sel, gcum) for gcum in gcums]

    pairs = [(b, h) for b in seqs for h in range(GROUP_HEADS)]
    qs, ks, kbs, gcs, egcs, decays, rhss = [], [], [], [], [], [], []
    for b, h in pairs:
        lo = h * HEAD_DIM
        y, sm = ys[b], sms[b]
        qh = y[:, lo:lo + HEAD_DIM]
        kh = y[:, GROUP_WIDTH + lo:GROUP_WIDTH + lo + HEAD_DIM]
        vh = y[:, 2 * GROUP_WIDTH + lo:2 * GROUP_WIDTH + lo + HEAD_DIM]
        qh = qh * lax.rsqrt(jnp.sum(qh * qh, axis=-1, keepdims=True) + EPS) * QK_SCALE
        kh = kh * lax.rsqrt(jnp.sum(kh * kh, axis=-1, keepdims=True) + EPS)
        if n_valid is not None:
            qh = jnp.where(valid, qh, 0.0)
            kh = jnp.where(valid, kh, 0.0)
            vh = jnp.where(valid, vh, 0.0)
        gc = gcums[b][:, SM_G + h:SM_G + h + 1]
        gr = grows[b][h:h + 1, :]
        bh = sm[:, SM_BETA + h:SM_BETA + h + 1]
        kb = kh * bh
        egc = jnp.exp(gc)
        qs.append(qh)
        ks.append(kh)
        kbs.append(kb)
        gcs.append(gc)
        egcs.append(egc)
        decays.append(jnp.exp(jnp.where(incl, gc - gr, NEG_BIG)))
        rhss.append(jnp.concatenate([vh * bh, kb * egc], axis=1))

    khbs = [kh.astype(BF16) for kh in ks]
    m_lows = [jnp.where(strict, _dot_nt(kb.astype(BF16), khb) * dec, 0.0) for kb, khb, dec in zip(kbs, khbs, decays)]
    a_qks = [(_dot_nt(qh.astype(BF16), khb) * dec).astype(BF16) for qh, khb, dec in zip(qs, khbs, decays)]
    g_lasts = [gc[c - 1:c, :] for gc in gcs]
    kdec_ts = [_dot_nt(eye_d, (kh * jnp.exp(gl - gc)).astype(BF16)).astype(BF16)
               for kh, gl, gc in zip(ks, g_lasts, gcs)]
    invs = _unit_lower_inverse(m_lows, c)
    sols = [_dot3(_dot, inv, rhs) for inv, rhs in zip(invs, rhss)]
    states = [st_ref[b, h] for b, h in pairs]
    sbs = [s.astype(BF16) for s in states]
    o_states = [_dot((qh * egc).astype(BF16), sb) for qh, egc, sb in zip(qs, egcs, sbs)]
    v_news = [sol[:, :HEAD_DIM] - _dot(sol[:, HEAD_DIM:].astype(BF16), sb) for sol, sb in zip(sols, sbs)]
    vnbs = [v.astype(BF16) for v in v_news]
    o_locals = [_dot(a, vnb) for a, vnb in zip(a_qks, vnbs)]
    s_adds = [_dot(kt, vnb) for kt, vnb in zip(kdec_ts, vnbs)]
    for i, (b, h) in enumerate(pairs):
        st_ref[b, h] = states[i] * jnp.exp(g_lasts[i]) + s_adds[i]
    for b in seqs:
        gate = gate_ref[b]
        outs = []
        for h in range(GROUP_HEADS):
            i, lo = b * GROUP_HEADS + h, h * HEAD_DIM
            o = _rms(o_states[i] + o_locals[i], hg_ref[:, lo:lo + HEAD_DIM]) * _silu(gate[:, lo:lo + HEAD_DIM])
            outs.append(o)
        o_ref[b] = jnp.concatenate(outs, axis=1)

    @pl.when(ci == n_chunks - 1)
    def _():
        sn_ref[...] = st_ref[...]


def _gated_deltanet(gqkv, small, ggate, conv_buf8, s0, conv_w, hg, *, c, n_valid):
    nb, t, _ = gqkv.shape
    g = math.gcd(GDN_BATCH, nb)
    hd, gh = HEAD_DIM, GROUP_HEADS
    return pl.pallas_call(
        functools.partial(_gdn_kernel, c=c, n_valid=n_valid),
        out_shape=[jax.ShapeDtypeStruct((nb, t, GROUP_WIDTH), F32),
                   jax.ShapeDtypeStruct((nb, gh, hd, hd), F32),
                   jax.ShapeDtypeStruct((nb, 8, GDN_CONV_DIM), F32)],
        grid=(nb // g, t // c),
        in_specs=[pl.BlockSpec((g, c, GDN_CONV_DIM), lambda b, i: (b, i, 0)),
                  pl.BlockSpec((g, c, SMALL_W), lambda b, i: (b, i, 0)),
                  pl.BlockSpec((g, c, GROUP_WIDTH), lambda b, i: (b, i, 0)),
                  pl.BlockSpec((g, 8, GDN_CONV_DIM), lambda b, i: (b, 0, 0)),
                  pl.BlockSpec((g, gh, hd, hd), lambda b, i: (b, 0, 0, 0)),
                  pl.BlockSpec((CONV_WIDTH, GDN_CONV_DIM), lambda b, i: (0, 0)),
                  pl.BlockSpec((1, GROUP_WIDTH), lambda b, i: (0, 0))],
        out_specs=[pl.BlockSpec((g, c, GROUP_WIDTH), lambda b, i: (b, i, 0)),
                   pl.BlockSpec((g, gh, hd, hd), lambda b, i: (b, 0, 0, 0)),
                   pl.BlockSpec((g, 8, GDN_CONV_DIM), lambda b, i: (b, 0, 0))],
        scratch_shapes=[pltpu.VMEM((g, 8, GDN_CONV_DIM), F32),
                        pltpu.VMEM((g, gh, hd, hd), F32)],
        compiler_params=_cparams(("arbitrary", "arbitrary")),
        name="gated_deltanet",
    )(gqkv, small, ggate, conv_buf8, s0, conv_w, hg)


def _gmlp_kernel(uv_ref, vg_ref, w_ref, bias_ref, hg_ref, o_ref, v_ref):
    x = uv_ref[0]
    gelu = x * (0.5 * (1.0 + jnp.tanh(math.sqrt(2.0 / math.pi) * (x + 0.044715 * (x * x * x)))))
    u = gelu[:, :GROUP_WIDTH]
    v = _rms(gelu[:, GROUP_WIDTH:], vg_ref[...])
    v_ref[0] = v
    vb = v.astype(BF16)
    n = x.shape[0]
    tri = _iota((n, n), 0) >= _iota((n, n), 1)
    lane_group = _iota((n, GROUP_WIDTH), 1) // HEAD_DIM
    mixed = bias_ref[...]
    for g in range(GROUP_HEADS):
        wg = jnp.where(tri, w_ref[g], 0.0).astype(BF16)
        mixed = mixed + jnp.where(lane_group == g, _dot(wg, vb), 0.0)
    o_ref[0] = _head_norm_lanes(u * mixed, hg_ref[...])


def _gmlp(guv, vg, w_s, bias_full, hg):
    nb, t, _ = guv.shape
    n = w_s.shape[-1]
    return pl.pallas_call(
        _gmlp_kernel,
        out_shape=[jax.ShapeDtypeStruct((nb, t, GROUP_WIDTH), F32),
                   jax.ShapeDtypeStruct((nb, t, GROUP_WIDTH), F32)],
        grid=(nb, t // n),
        in_specs=[pl.BlockSpec((1, n, 2 * GROUP_WIDTH), lambda b, i: (b, i, 0)),
                  pl.BlockSpec((1, GROUP_WIDTH), lambda b, i: (0, 0)),
                  pl.BlockSpec((GROUP_HEADS, n, n), lambda b, i: (0, 0, 0)),
                  pl.BlockSpec((n, GROUP_WIDTH), lambda b, i: (0, 0)),
                  pl.BlockSpec((1, GROUP_WIDTH), lambda b, i: (0, 0))],
        out_specs=[pl.BlockSpec((1, n, GROUP_WIDTH), lambda b, i: (b, i, 0)),
                   pl.BlockSpec((1, n, GROUP_WIDTH), lambda b, i: (b, i, 0))],
        compiler_params=_cparams(("arbitrary", "arbitrary")),
        name="gmlp_gating",
    )(guv, vg, w_s, bias_full, hg)


def _post_kernel(of_ref, os_ref, og_ref, om_ref, x_ref, g1_ref, sh2_ref, sc2_ref, g2_ref, ng_ref,
                 wo_ref, w1_ref, w2_ref, y_ref, x1_ref, h_ref, acc_ref):
    f = pl.program_id(2)

    @pl.when(f == 0)
    def _():
        m = _dot(of_ref[0].astype(BF16), wo_ref[0 * GROUP_WIDTH:1 * GROUP_WIDTH, :])
        m = m + _dot(os_ref[0].astype(BF16), wo_ref[1 * GROUP_WIDTH:2 * GROUP_WIDTH, :])
        m = m + _dot(og_ref[0].astype(BF16), wo_ref[2 * GROUP_WIDTH:3 * GROUP_WIDTH, :])
        m = m + _dot(om_ref[0].astype(BF16), wo_ref[3 * GROUP_WIDTH:4 * GROUP_WIDTH, :])
        x1 = x_ref[0] + g1_ref[0] * _rms(m, ng_ref[1:2, :])
        x1_ref[...] = x1
        h = _rms(x1, ng_ref[2:3, :]) * (1.0 + sc2_ref[0]) + sh2_ref[0]
        h_ref[...] = h.astype(BF16)
        acc_ref[...] = jnp.zeros_like(acc_ref)

    a = jnp.maximum(_dot(h_ref[...], w1_ref[...]), 0.0)
    acc_ref[...] += _dot((a * a).astype(BF16), w2_ref[...])

    @pl.when(f == pl.num_programs(2) - 1)
    def _():
        y_ref[0] = x1_ref[...] + g2_ref[0] * _rms(acc_ref[...], ng_ref[3:4, :])


def _post(o_fox, o_sb, o_gdn, o_gmlp, x, gate1, shift2, scale2, gate2, norm_g, w_out, w1, w2):
    nb, t, d = x.shape
    dff = w1.shape[1]
    tm = min(MLP_ROW_TILE, t)
    tf = min(512, dff)
    tmod = gate1.shape[1]
    bm = tm if tmod == t else 1
    mod_spec = pl.BlockSpec((1, bm, d), (lambda b, i, f: (b, i, 0)) if tmod == t else (lambda b, i, f: (b, 0, 0)))
    o_spec = pl.BlockSpec((1, tm, GROUP_WIDTH), lambda b, i, f: (b, i, 0))
    return pl.pallas_call(
        _post_kernel,
        out_shape=jax.ShapeDtypeStruct((nb, t, d), F32),
        grid=(nb, t // tm, dff // tf),
        in_specs=[o_spec, o_spec, o_spec, o_spec,
                  pl.BlockSpec((1, tm, d), lambda b, i, f: (b, i, 0)),
                  mod_spec, mod_spec, mod_spec, mod_spec,
                  pl.BlockSpec((4, d), lambda b, i, f: (0, 0)),
                  pl.BlockSpec((d, d), lambda b, i, f: (0, 0)),
                  pl.BlockSpec((d, tf), lambda b, i, f: (0, f)),
                  pl.BlockSpec((tf, d), lambda b, i, f: (f, 0))],
        out_specs=pl.BlockSpec((1, tm, d), lambda b, i, f: (b, i, 0)),
        scratch_shapes=[pltpu.VMEM((tm, d), F32), pltpu.VMEM((tm, d), BF16), pltpu.VMEM((tm, d), F32)],
        compiler_params=_cparams(("arbitrary", "arbitrary", "arbitrary")),
        name="outproj_mlp",
    )(o_fox, o_sb, o_gdn, o_gmlp, x, gate1, shift2, scale2, gate2, norm_g, w_out, w1, w2)


def _bias_rows(n_pages):
    return GROUP_HEADS * (n_pages + 2)


def _dec_bias_kernel(pt_ref, cache_ref, new_ref, later_ref, o_ref, buf_ref, sem, *, layer, n_pages):
    nb = o_ref.shape[0]
    rows = _bias_rows(n_pages)
    total = nb * n_pages

    def page_copy(i):
        b = i // n_pages
        p = i - b * n_pages
        dst = pl.multiple_of(b * rows + GROUP_HEADS * p, GROUP_HEADS)
        return pltpu.make_async_copy(cache_ref.at[layer, pt_ref[b, p]],
                                     buf_ref.at[pl.ds(dst, GROUP_HEADS), :], sem)

    def start(i, c):
        page_copy(i).start()
        return c

    def wait(i, c):
        page_copy(i).wait()
        return c

    lax.fori_loop(0, total, start, 0)
    lax.fori_loop(0, total, wait, 0)

    after = _tri01(PAGE_SIZE, strict=True, lower=True)
    later = later_ref[...]
    for b in range(nb):
        buf_ref[b * rows + GROUP_HEADS * n_pages:(b + 1) * rows, :] = new_ref[b]
        x = buf_ref[b * rows:(b + 1) * rows, :]
        within = _dot_exact_r(_dot, x, after)
        tot = jnp.broadcast_to(jnp.sum(x, axis=1, keepdims=True), x.shape)
        o_ref[b] = within + _dot_exact_l(_dot, later, tot)


def _dec_bias(page_table, cache_logf_t, new_rows, later, *, layer):
    nb, n_pages = page_table.shape
    rows = _bias_rows(n_pages)
    return pl.pallas_call(
        functools.partial(_dec_bias_kernel, layer=layer, n_pages=n_pages),
        out_shape=jax.ShapeDtypeStruct((nb, rows, PAGE_SIZE), F32),
        grid_spec=pltpu.PrefetchScalarGridSpec(
            num_scalar_prefetch=1,
            grid=(1,),
            in_specs=[pl.BlockSpec(memory_space=pl.ANY),
                      pl.BlockSpec((nb, 2 * GROUP_HEADS, PAGE_SIZE), lambda i, pt: (0, 0, 0)),
                      pl.BlockSpec((rows, rows), lambda i, pt: (0, 0))],
            out_specs=pl.BlockSpec((nb, rows, PAGE_SIZE), lambda i, pt: (0, 0, 0)),
            scratch_shapes=[pltpu.VMEM((nb * rows, PAGE_SIZE), F32), pltpu.SemaphoreType.DMA(())]),
        compiler_params=_cparams(("arbitrary",)),
        name="decode_forget_bias",
    )(page_table, cache_logf_t, new_rows, later)


def _rows_by_head(per_head, rows):
    per = rows // GROUP_HEADS
    rg = _iota((rows, PAGE_SIZE), 0) // per
    out = jnp.broadcast_to(per_head[0:1, :], (rows, PAGE_SIZE))
    for h in range(1, GROUP_HEADS):
        out = jnp.where(rg == h, jnp.broadcast_to(per_head[h:h + 1, :], (rows, PAGE_SIZE)), out)
    return out


def _dec_attn_kernel(pt_ref, *refs, pp, n_pages, n_q):
    del pt_ref
    caches = refs[:4 * pp]
    (qf_ref, qs_ref, nfk_ref, nfv_ref, nsk_ref, nsv_ref, bias_ref, hgf_ref, hgs_ref,
     of_ref, os_ref, qbf_ref, qbs_ref, m_ref, l_ref, accf_ref, run_ref, accs_ref, ft_ref) = refs[4 * pp:]
    g = pl.program_id(1)
    rows = GROUP_HEADS * n_q
    row_head = _iota((rows, GROUP_WIDTH), 0) // n_q
    lane_head = _iota((rows, GROUP_WIDTH), 1) // HEAD_DIM
    own = row_head == lane_head
    later = _tri01(PAGE_SIZE, strict=True, lower=True)

    def attend(kfs, vfs, biases, fox_mask, kss, vss, sb_mask, transposed):
        qk, pv = (_dot, _dot_nt) if transposed else (_dot_nt, _dot)
        n = len(kfs)
        qf, qs = qbf_ref[...], qbs_ref[...]
        s_parts = [qk(qf, kf) for kf in kfs]
        zs = [qk(qs, ks) for ks in kss]

        s = jnp.concatenate(s_parts, axis=1) + (jnp.concatenate(biases, axis=1) - ft_ref[...])
        if fox_mask is not None:
            s = jnp.where(fox_mask, s, NEG_BIG)
        m = m_ref[...]
        m_new = jnp.maximum(m, jnp.max(s, axis=1, keepdims=True))
        alpha = jnp.exp(m - m_new)
        p = jnp.exp(s - m_new)
        l_ref[...] = alpha * l_ref[...] + jnp.sum(p, axis=1, keepdims=True)
        m_ref[...] = m_new

        log_keeps = []
        for z in zs:
            log_keep = -(jnp.maximum(z, 0.0) + _softplus_parts(z))
            if sb_mask is not None:
                log_keep = jnp.where(sb_mask, log_keep, 0.0)
            log_keeps.append(log_keep)
        afters = [_dot_exact_r(_dot, lk, later, pieces=2) for lk in log_keeps]
        fox_parts = [pv(p[:, r * PAGE_SIZE:(r + 1) * PAGE_SIZE].astype(BF16), vfs[r]) for r in range(n)]

        run = run_ref[...]
        wgts = []
        for r in range(n):
            wgt = jnp.exp(log_keeps[r] + zs[r] + (afters[r] + run))
            if sb_mask is not None:
                wgt = jnp.where(sb_mask, wgt, 0.0)
            wgts.append(wgt.astype(BF16))
            run = run + jnp.sum(log_keeps[r], axis=1, keepdims=True)
        run_ref[...] = run
        sb_parts = [pv(wgts[r], vss[r]) for r in range(n)]

        accf_ref[...] = alpha * accf_ref[...] + functools.reduce(lambda a, b: a + b, fox_parts)
        accs_ref[...] = accs_ref[...] + functools.reduce(lambda a, b: a + b, sb_parts)

    @pl.when(g == 0)
    def _():
        qbf_ref[...] = jnp.where(own, qf_ref[0] * QK_SCALE, 0.0).astype(BF16)
        qbs_ref[...] = jnp.where(own, qs_ref[0] * QK_SCALE, 0.0).astype(BF16)
        m_ref[...] = jnp.full_like(m_ref, NEG_BIG)
        l_ref[...] = jnp.zeros_like(l_ref)
        accf_ref[...] = jnp.zeros_like(accf_ref)
        accs_ref[...] = jnp.zeros_like(accs_ref)
        run_ref[...] = jnp.zeros_like(run_ref)
        bias16 = _rows_by_head(bias_ref[0, GROUP_HEADS * n_pages:GROUP_HEADS * (n_pages + 1), :], rows)
        qpos = _iota((rows, PAGE_SIZE), 0) % n_q
        kpos = _iota((rows, PAGE_SIZE), 1)
        ft_ref[...] = jnp.sum(jnp.where(kpos == qpos, bias16, 0.0), axis=1, keepdims=True)
        pad = jnp.zeros((PAGE_SIZE - 8, GROUP_WIDTH), F32)

        def page_of(r):
            return jnp.concatenate([r[0], pad], axis=0).astype(BF16)

        attend([page_of(nfk_ref)], [page_of(nfv_ref)], [bias16], (kpos <= qpos) & (kpos < n_q),
               [page_of(nsk_ref)], [page_of(nsv_ref)], kpos < qpos, False)

    @pl.when(g > 0)
    def _():
        first = pl.multiple_of(GROUP_HEADS * (n_pages - g * pp), GROUP_HEADS * pp)
        slab = bias_ref[0, pl.ds(first, GROUP_HEADS * pp), :]
        biases = [_rows_by_head(slab[GROUP_HEADS * (pp - 1 - r):GROUP_HEADS * (pp - r), :], rows) for r in range(pp)]
        page = lambda r, i: caches[4 * r + i][0, 0].astype(BF16)
        attend([page(r, 0) for r in range(pp)], [page(r, 1) for r in range(pp)], biases, None,
               [page(r, 2) for r in range(pp)], [page(r, 3) for r in range(pp)], None, True)

    @pl.when(g == pl.num_programs(1) - 1)
    def _():
        def fold(acc):
            a = jnp.where(own, acc, 0.0)
            out = a[0:n_q, :]
            for h in range(1, GROUP_HEADS):
                out = out + a[h * n_q:(h + 1) * n_q, :]
            return out

        of_ref[0] = _head_norm_lanes(fold(accf_ref[...] / l_ref[...]), hgf_ref[...])
        os_ref[0] = _head_norm_lanes(fold(accs_ref[...]), hgs_ref[...])


def _dec_attention(page_table, cfk, cfv, csk, csv, qf16, qs16, nfk, nfv, nsk, nsv, bias, hgf, hgs, *, layer, n_q):
    nb, n_pages = page_table.shape
    pp = math.gcd(DEC_PAGES, n_pages)
    assert pp % 2 == 0
    n_groups = n_pages // pp
    rows = GROUP_HEADS * n_q

    def cache_spec(r):
        def imap(b, g, pt):
            page = n_pages - 1 - (jnp.maximum(g - 1, 0) * pp + r)
            return (layer, pt[b, page], 0, 0)
        return pl.BlockSpec((1, 1, GROUP_WIDTH, PAGE_SIZE), imap)

    cache_specs, cache_args = [], []
    for r in range(pp):
        for arr in (cfk, cfv, csk, csv):
            cache_specs.append(cache_spec(r))
            cache_args.append(arr)
    per_b3 = lambda b, g, pt: (b, 0, 0)
    const2 = lambda b, g, pt: (0, 0)
    return pl.pallas_call(
        functools.partial(_dec_attn_kernel, pp=pp, n_pages=n_pages, n_q=n_q),
        out_shape=[jax.ShapeDtypeStruct((nb, n_q, GROUP_WIDTH), F32)] * 2,
        grid_spec=pltpu.PrefetchScalarGridSpec(
            num_scalar_prefetch=1,
            grid=(nb, n_groups + 1),
            in_specs=cache_specs + [
                pl.BlockSpec((1, rows, GROUP_WIDTH), per_b3),
                pl.BlockSpec((1, rows, GROUP_WIDTH), per_b3),
                pl.BlockSpec((1, 8, GROUP_WIDTH), per_b3),
                pl.BlockSpec((1, 8, GROUP_WIDTH), per_b3),
                pl.BlockSpec((1, 8, GROUP_WIDTH), per_b3),
                pl.BlockSpec((1, 8, GROUP_WIDTH), per_b3),
                pl.BlockSpec((1, _bias_rows(n_pages), PAGE_SIZE), per_b3),
                pl.BlockSpec((1, GROUP_WIDTH), const2),
                pl.BlockSpec((1, GROUP_WIDTH), const2)],
            out_specs=[pl.BlockSpec((1, n_q, GROUP_WIDTH), per_b3)] * 2,
            scratch_shapes=[pltpu.VMEM((rows, GROUP_WIDTH), BF16), pltpu.VMEM((rows, GROUP_WIDTH), BF16),
                            pltpu.VMEM((rows, 1), F32), pltpu.VMEM((rows, 1), F32),
                            pltpu.VMEM((rows, GROUP_WIDTH), F32),
                            pltpu.VMEM((rows, 1), F32), pltpu.VMEM((rows, GROUP_WIDTH), F32),
                            pltpu.VMEM((rows, 1), F32)]),
        compiler_params=_cparams(("arbitrary", "arbitrary")),
        name="decode_attention",
    )(page_table, *cache_args, qf16, qs16, nfk, nfv, nsk, nsv, bias, hgf, hgs)


def _permute_w_in(w_in):
    gw, gh = GROUP_WIDTH, GROUP_HEADS
    o = 0
    fq = w_in[:, o:o + gw]; o += gw
    fk = w_in[:, o:o + gw]; o += gw
    fv = w_in[:, o:o + gw]; o += gw
    ff = w_in[:, o:o + gh]; o += gh
    sq = w_in[:, o:o + gw]; o += gw
    sk = w_in[:, o:o + gw]; o += gw
    sv = w_in[:, o:o + gw]; o += gw
    gqkv = w_in[:, o:o + GDN_CONV_DIM]; o += GDN_CONV_DIM
    ga = w_in[:, o:o + gh]; o += gh
    gb = w_in[:, o:o + gh]; o += gh
    ggate = w_in[:, o:o + gw]; o += gw
    guv = w_in[:, o:o + 2 * gw]
    pad = jnp.zeros((w_in.shape[0], SMALL_W - 4 * gh), w_in.dtype)
    return jnp.concatenate([fq, fk, fv, sq, sk, sv, gqkv, ggate, guv, ff, ga, gb, ff, pad], axis=1).astype(BF16)


def _small_params(b_forget, a_log, dt_bias):
    z4 = jnp.zeros((GROUP_HEADS,), F32)
    zpad = jnp.zeros((SMALL_W - 4 * GROUP_HEADS,), F32)
    bias = jnp.concatenate([b_forget, dt_bias, z4, b_forget, zpad])
    alog = jnp.concatenate([z4, a_log, z4, z4, zpad])
    return jnp.zeros((8, SMALL_W), F32).at[0].set(bias).at[1].set(alog)


def _later_pages_matrix(n_pages):
    idx = jnp.arange(_bias_rows(n_pages))
    page, head = idx // GROUP_HEADS, idx % GROUP_HEADS
    return ((head[:, None] == head[None, :]) & (page[None, :] > page[:, None])).astype(BF16)


def _pages_transposed(cache):
    depth, n_phys = cache.shape[:2]
    return jnp.transpose(cache, (0, 1, 3, 4, 2)).reshape(depth, n_phys, GROUP_WIDTH, PAGE_SIZE)


def _heads_last(a_t, t):
    return jnp.transpose(a_t.reshape(a_t.shape[0], GROUP_HEADS, HEAD_DIM, t), (0, 3, 1, 2))


def kernel(x_prompt, x_sample, cache_fox_k, cache_fox_v, cache_fox_logf, cache_sb_k, cache_sb_v, state_gdn, state_gdn_conv, page_table, c_prompt, c_sample, ada_w, ada_b, norm_g, w_in, b_forget, gdn_conv_w, gdn_a_log, gdn_dt_bias, gmlp_v_norm_g, gmlp_w_s, gmlp_b_s, head_norm_g, w_out, w_ff1, w_ff2):
    depth = ada_w.shape[0]
    n_p, t_p, d = x_prompt.shape
    n_s, t_s, _ = x_sample.shape
    n_phys = cache_fox_k.shape[1]
    rows_s = n_s * t_s
    gw, gh = GROUP_WIDTH, GROUP_HEADS

    mods = _modulation(jnp.concatenate([c_prompt, c_sample], axis=0), ada_w, ada_b)
    mods_p = mods[:, :n_p].reshape(depth, n_p, 1, N_MOD, d)
    mods_s = jnp.repeat(mods[:, n_p:], t_s, axis=1).reshape(depth, 1, rows_s, N_MOD, d)

    n_pages = page_table.shape[1]
    cfk, cfv = _pages_transposed(cache_fox_k), _pages_transposed(cache_fox_v)
    csk, csv = _pages_transposed(cache_sb_k), _pages_transposed(cache_sb_v)
    clogf = jnp.swapaxes(cache_fox_logf, 2, 3)
    later = _later_pages_matrix(n_pages)

    zeros_cb = jnp.zeros((n_p, 8, GDN_CONV_DIM), F32)
    zeros_s0 = jnp.zeros((n_p, gh, HEAD_DIM, HEAD_DIM), F32)
    eye_s = jnp.eye(n_s, dtype=F32)

    x_p = x_prompt
    x_s = x_sample.reshape(1, rows_s, d)
    outs_p, outs_s = [], []
    for l in range(depth):
        w_l = _permute_w_in(w_in[l])
        wt_l = jnp.concatenate([w_l[:, C_FK:C_FK + 2 * gw], w_l[:, C_SK:C_SK + 2 * gw]], axis=1).T
        sp = _small_params(b_forget[l], gdn_a_log[l], gdn_dt_bias[l])
        hg = head_norm_g[l].reshape(4, 1, gw)
        wo, w1, w2 = w_out[l].astype(BF16), w_ff1[l].astype(BF16), w_ff2[l].astype(BF16)
        ng = norm_g[l]
        bias_p = jnp.repeat(gmlp_b_s[l].T, HEAD_DIM, axis=1)
        vg = gmlp_v_norm_g[l].reshape(1, gw)

        mp = [mods_p[l, :, :, i] for i in range(N_MOD)]
        fq, fk, fv, sq, sk, sv, gqkv, ggate, guv, small = _in_projection(
            x_p, mp[0], mp[1], ng[0:1], w_l, wt_l, sp, cumsum=True, kv_transposed=True)
        frow = jnp.swapaxes(small[:, :, SM_CUM:SM_CUM + gh], 1, 2)
        o_fox = _fox_prompt(fq, fk, fv, small, frow, hg[0])
        o_sb = _sb_prompt(sq, sk, sv, hg[1])
        o_gdn, s_new, cb_new = _gated_deltanet(gqkv, small, ggate, zeros_cb, zeros_s0, gdn_conv_w[l], hg[2],
                                               c=GDN_CHUNK, n_valid=None)
        o_gmlp, _ = _gmlp(guv, vg, gmlp_w_s[l], bias_p, hg[3])
        x_p = _post(o_fox, o_sb, o_gdn, o_gmlp, x_p, mp[2], mp[3], mp[4], mp[5], ng, wo, w1, w2)
        outs_p.append((_heads_last(fk, t_p), _heads_last(fv, t_p), small[:, :, SM_LOGF:SM_LOGF + gh],
                       _heads_last(sk, t_p), _heads_last(sv, t_p), s_new, cb_new[:, 8 - (CONV_WIDTH - 1):]))

        ms = [mods_s[l, :, :, i] for i in range(N_MOD)]
        fq, fk, fv, sq, sk, sv, gqkv, ggate, guv, small = _in_projection(
            x_s, ms[0], ms[1], ng[0:1], w_l, wt_l, sp, cumsum=False, kv_transposed=False)
        logf_new = small[0, :, SM_LOGF:SM_LOGF + gh]
        new_rows = jnp.pad(jnp.swapaxes(logf_new.reshape(n_s, t_s, gh), 1, 2),
                           ((0, 0), (0, gh), (0, PAGE_SIZE - t_s)))
        bias = _dec_bias(page_table, clogf, new_rows, later, layer=l)
        per_seq = lambda a: a.reshape(n_s, t_s, gw)
        pad8 = lambda a: jnp.pad(per_seq(a), ((0, 0), (0, 8 - t_s), (0, 0)))
        tile_q = lambda a: jnp.tile(per_seq(a), (1, gh, 1))
        o_fox, o_sb = _dec_attention(page_table, cfk, cfv, csk, csv, tile_q(fq), tile_q(sq),
                                     pad8(fk), pad8(fv), pad8(sk), pad8(sv), bias, hg[0], hg[1],
                                     layer=l, n_q=t_s)
        cs = GDN_CHUNK_SMALL
        padc = lambda a: jnp.pad(a.reshape(n_s, t_s, a.shape[-1]), ((0, 0), (0, cs - t_s), (0, 0)))
        cb8 = jnp.pad(state_gdn_conv[l], ((0, 0), (8 - (CONV_WIDTH - 1), 0), (0, 0)))
        o_gdn, s_new, cb_new = _gated_deltanet(padc(gqkv), padc(small), padc(ggate), cb8, state_gdn[l],
                                               gdn_conv_w[l], hg[2], c=cs, n_valid=t_s)
        o_gdn = o_gdn[:, :t_s].reshape(1, rows_s, gw)
        w_blk = jnp.einsum('ab,gij->gaibj', eye_s, gmlp_w_s[l][:, :t_s, :t_s]).reshape(gh, rows_s, rows_s)
        bias_s = jnp.tile(bias_p[:t_s], (n_s, 1))
        o_gmlp, v_rows = _gmlp(guv, vg, w_blk, bias_s, hg[3])
        x_s = _post(o_fox.reshape(1, rows_s, gw), o_sb.reshape(1, rows_s, gw), o_gdn, o_gmlp, x_s,
                    ms[2], ms[3], ms[4], ms[5], ng, wo, w1, w2)
        outs_s.append((per_seq(fk[0]), per_seq(fv[0]), logf_new.reshape(n_s, t_s, gh), per_seq(sk[0]),
                       per_seq(sv[0]), s_new, cb_new[:, 8 - (CONV_WIDTH - 1):], per_seq(v_rows[0])))

    def stacked(states, i, shape=None):
        a = jnp.stack([s[i] for s in states])
        return a if shape is None else a.reshape(shape)

    hp = (depth, n_p, t_p, gh, HEAD_DIM)
    hs = (depth, n_s, t_s, gh, HEAD_DIM)
    return (x_p, x_s.reshape(n_s, t_s, d),
            stacked(outs_p, 0, hp), stacked(outs_p, 1, hp), stacked(outs_p, 2), stacked(outs_p, 3, hp),
            stacked(outs_p, 4, hp), stacked(outs_p, 5), stacked(outs_p, 6),
            stacked(outs_s, 0, hs), stacked(outs_s, 1, hs), stacked(outs_s, 2), stacked(outs_s, 3, hs),
            stacked(outs_s, 4, hs), stacked(outs_s, 5), stacked(outs_s, 6), stacked(outs_s, 7))
```

```python
import functools
import math

import jax
import jax.numpy as jnp
from jax import lax
from jax.experimental import pallas as pl
from jax.experimental.pallas import tpu as pltpu

F32 = jnp.float32
BF16 = jnp.bfloat16

HEAD_DIM = 64
GROUP_HEADS = 4
GROUP_WIDTH = GROUP_HEADS * HEAD_DIM
CONV_WIDTH = 4
GDN_CONV_DIM = 3 * GROUP_WIDTH
GDN_CHUNK = 64
GDN_CHUNK_SMALL = 16
MLP_CHUNK = 128
PAGE_SIZE = 128
N_MOD = 6
EPS = 1e-6
QK_SCALE = HEAD_DIM ** -0.5
NEG_BIG = -1e30
LOG2_E = 1.4426950408889634

SM_LOGF = 0
SM_G = 4
SM_BETA = 8
SM_CUM = 12
SMALL_W = 128

C_FQ, C_FK, C_FV, C_SQ, C_SK, C_SV = (i * GROUP_WIDTH for i in range(6))
C_GQKV = 6 * GROUP_WIDTH
C_GGATE = C_GQKV + GDN_CONV_DIM
C_GUV = C_GGATE + GROUP_WIDTH
C_SMALL = C_GUV + 2 * GROUP_WIDTH
IN_PAD_WIDTH = C_SMALL + SMALL_W

VMEM_LIMIT = 56 * 1024 * 1024

ATTN_Q_TILE = 512
SB_KEY_TILE = 256
GDN_BATCH = 8
DEC_PAGES = 8
GMLP_CHUNKS = 4
MLP_ROW_TILE = 1024


def _cparams(sem):
    return pltpu.CompilerParams(dimension_semantics=sem, vmem_limit_bytes=VMEM_LIMIT)


def _dot(a, b):
    return jnp.dot(a, b, preferred_element_type=F32)


def _dot_nt(a, b):
    return lax.dot_general(a, b, (((1,), (1,)), ((), ())), preferred_element_type=F32)


def _split2(a):
    hi = a.astype(BF16)
    lo = (a - hi.astype(F32)).astype(BF16)
    return hi, lo


def _split3(a):
    hi = a.astype(BF16)
    r = a - hi.astype(F32)
    mid = r.astype(BF16)
    lo = (r - mid.astype(F32)).astype(BF16)
    return hi, mid, lo


def _dot3(dot, a, b):
    ah, al = _split2(a)
    bh, bl = _split2(b)
    return dot(ah, bh) + (dot(ah, bl) + dot(al, bh))


def _dot_exact_l(dot, a01, b, pieces=3):
    parts = _split3(b) if pieces == 3 else _split2(b)
    out = dot(a01, parts[0])
    for p in parts[1:]:
        out = out + dot(a01, p)
    return out


def _split2_trunc(a):
    hi = pltpu.bitcast(pltpu.bitcast(a, jnp.uint32) & jnp.uint32(0xFFFF0000), F32)
    return hi.astype(BF16), (a - hi).astype(BF16)


def _dot_exact_r(dot, a, b01, pieces=3):
    parts = _split3(a) if pieces == 3 else _split2_trunc(a)
    out = dot(parts[0], b01)
    for p in parts[1:]:
        out = out + dot(p, b01)
    return out


def _softplus_parts(x):
    return jnp.log(1.0 + jnp.exp2(jnp.abs(x) * (-LOG2_E)))


def _sigmoid(x):
    return 1.0 / (1.0 + jnp.exp(-x))


def _silu(x):
    return x * _sigmoid(x)


def _rms(x, g):
    return x * lax.rsqrt(jnp.mean(x * x, axis=-1, keepdims=True) + EPS) * g


def _iota(shape, dim):
    return lax.broadcasted_iota(jnp.int32, shape, dim)


def _tri01(n, strict, lower, dtype=BF16):
    r, c = _iota((n, n), 0), _iota((n, n), 1)
    if lower:
        m = (r > c) if strict else (r >= c)
    else:
        m = (r < c) if strict else (r <= c)
    return jnp.where(m, 1.0, 0.0).astype(dtype)


def _group_ones(n, group):
    r, c = _iota((n, n), 0), _iota((n, n), 1)
    return jnp.where((r // group) == (c // group), 1.0, 0.0).astype(BF16)


def _head_norm_lanes(x, gain):
    ones = _group_ones(x.shape[-1], HEAD_DIM)
    ms = _dot_exact_r(_dot, x * x, ones) * (1.0 / HEAD_DIM)
    return x * lax.rsqrt(ms + EPS) * gain


def _mod_kernel(c_ref, w_ref, b_ref, o_ref):
    a = _silu(c_ref[...]).astype(BF16)
    o_ref[0] = _dot(a, w_ref[0].astype(BF16)) + b_ref[0]


def _modulation(c_all, ada_w, ada_b):
    depth, d, n = ada_w.shape
    rows = c_all.shape[0]
    tn = 1536 if n % 1536 == 0 else n
    return pl.pallas_call(
        _mod_kernel,
        out_shape=jax.ShapeDtypeStruct((depth, rows, n), F32),
        grid=(depth, n // tn),
        in_specs=[pl.BlockSpec((rows, d), lambda l, j: (0, 0)),
                  pl.BlockSpec((1, d, tn), lambda l, j: (l, 0, j)),
                  pl.BlockSpec((1, 1, tn), lambda l, j: (l, 0, j))],
        out_specs=pl.BlockSpec((1, rows, tn), lambda l, j: (l, 0, j)),
        compiler_params=_cparams(("arbitrary", "arbitrary")),
        name="adaln_mod",
    )(c_all, ada_w, ada_b.reshape(depth, 1, n))


def _inproj_kernel(x_ref, sh_ref, sc_ref, g_ref, w_ref, wt_ref, sp_ref,
                   fq, fk, fv, sq, sk, sv, gqkv, ggate, guv, small, carry_ref, *, cumsum, kv_transposed):
    x = x_ref[0]
    h = _rms(x, g_ref[...]) * (1.0 + sc_ref[0]) + sh_ref[0]
    hb = h.astype(BF16)

    def seg(off, n):
        return _dot(hb, w_ref[:, off:off + n])

    def seg_t(i):
        return _dot_nt(wt_ref[i * GROUP_WIDTH:(i + 1) * GROUP_WIDTH, :], hb)

    fq[0] = seg(C_FQ, GROUP_WIDTH)
    sq[0] = seg(C_SQ, GROUP_WIDTH)
    if kv_transposed:
        fk[0] = seg_t(0)
        fv[0] = seg_t(1)
        sk[0] = seg_t(2)
        sv[0] = seg_t(3)
    else:
        fk[0] = seg(C_FK, GROUP_WIDTH)
        fv[0] = seg(C_FV, GROUP_WIDTH)
        sk[0] = seg(C_SK, GROUP_WIDTH)
        sv[0] = seg(C_SV, GROUP_WIDTH)
    gqkv[0] = seg(C_GQKV, GDN_CONV_DIM)
    ggate[0] = seg(C_GGATE, GROUP_WIDTH)
    guv[0] = seg(C_GUV, 2 * GROUP_WIDTH)

    z = seg(C_SMALL, SMALL_W)
    tm = z.shape[0]
    lane = _iota(z.shape, 1)
    zz = z + sp_ref[0:1, :]
    t = _softplus_parts(zz)
    logf = jnp.minimum(zz, 0.0) - t
    gdec = -jnp.exp(sp_ref[1:2, :]) * (jnp.maximum(zz, 0.0) + t)
    beta = _sigmoid(z)
    out = jnp.where(lane < SM_G, logf, jnp.where(lane < SM_BETA, gdec, jnp.where(lane < SM_CUM, beta, 0.0)))
    if cumsum:
        @pl.when(pl.program_id(1) == 0)
        def _():
            carry_ref[...] = jnp.zeros_like(carry_ref)

        lf = jnp.where((lane >= SM_CUM) & (lane < SM_CUM + GROUP_HEADS), logf, 0.0)
        cum = _dot_exact_l(_dot, _tri01(tm, strict=False, lower=True), lf) + carry_ref[0:1, :]
        carry_ref[0:1, :] = cum[tm - 1:tm, :]
        out = out + cum
    small[0] = out


def _in_projection(x, shift, scale, g, w, wt, sp, *, layer, cumsum, kv_transposed):
    nb, t, d = x.shape
    tm = min(512, t)
    tmod = shift.shape[1]
    bm = tm if tmod == t else 1
    mod_spec = pl.BlockSpec((1, bm, d), (lambda b, i: (b, i, 0)) if tmod == t else (lambda b, i: (b, 0, 0)))
    gw = GROUP_WIDTH
    row = lambda wd: (jax.ShapeDtypeStruct((nb, t, wd), F32), pl.BlockSpec((1, tm, wd), lambda b, i: (b, i, 0)))
    col = (jax.ShapeDtypeStruct((nb, gw, t), F32), pl.BlockSpec((1, gw, tm), lambda b, i: (b, 0, i)))
    kv = col if kv_transposed else row(gw)
    outs = [row(gw), kv, kv, row(gw), kv, kv, row(GDN_CONV_DIM), row(gw), row(2 * gw), row(SMALL_W)]
    return pl.pallas_call(
        functools.partial(_inproj_kernel, cumsum=cumsum, kv_transposed=kv_transposed),
        out_shape=[o[0] for o in outs],
        grid=(nb, t // tm),
        in_specs=[pl.BlockSpec((1, tm, d), lambda b, i: (b, i, 0)),
                  mod_spec, mod_spec,
                  pl.BlockSpec((1, d), lambda b, i: (0, 0)),
                  pl.BlockSpec((None, d, IN_PAD_WIDTH), lambda b, i: (layer, 0, 0)),
                  pl.BlockSpec((None, 4 * gw, d), lambda b, i: (layer, 0, 0)),
                  pl.BlockSpec((8, SMALL_W), lambda b, i: (0, 0))],
        out_specs=[o[1] for o in outs],
        scratch_shapes=[pltpu.VMEM((8, SMALL_W), F32)],
        compiler_params=_cparams(("arbitrary", "arbitrary")),
        name="in_projection",
    )(x, shift, scale, g, w, wt, sp)


def _fox_kernel(q_ref, kt_ref, vt_ref, smq_ref, frow_ref, hg_ref, o_ref, ks_ref, vs_ref, *, tq):
    qi = pl.program_id(1)

    @pl.when(qi == 0)
    def _():
        ks_ref[...] = kt_ref[0].astype(BF16)
        vs_ref[...] = vt_ref[0].astype(BF16)

    q_all = q_ref[0]
    sm = smq_ref[0]
    qhs = [(q_all[:, h * HEAD_DIM:(h + 1) * HEAD_DIM] * QK_SCALE).astype(BF16) for h in range(GROUP_HEADS)]
    f_qs = [sm[:, SM_CUM + h:SM_CUM + h + 1] for h in range(GROUP_HEADS)]

    def step(j, carry, masked):
        start = pl.multiple_of(j * tq, tq)
        heads = range(GROUP_HEADS)
        rows = [slice(h * HEAD_DIM, (h + 1) * HEAD_DIM) for h in heads]
        qk = [_dot(qhs[h], ks_ref[rows[h], pl.ds(start, tq)]) for h in heads]
        stats, ps = [], []
        for h in heads:
            m, l, _ = carry[h]
            f_k = frow_ref[0, h:h + 1, pl.ds(start, tq)]
            s = qk[h] + (f_qs[h] - f_k)
            if masked:
                s = jnp.where(_iota((tq, tq), 1) <= _iota((tq, tq), 0), s, NEG_BIG)
            m_new = jnp.maximum(m, jnp.max(s, axis=1, keepdims=True))
            alpha = jnp.exp(m - m_new)
            p = jnp.exp(s - m_new)
            stats.append((m_new, alpha, alpha * l + jnp.sum(p, axis=1, keepdims=True)))
            ps.append(p.astype(BF16))
        pv = [_dot_nt(ps[h], vs_ref[rows[h], pl.ds(start, tq)]) for h in heads]
        return tuple((stats[h][0], stats[h][2], stats[h][1] * carry[h][2] + pv[h]) for h in heads)

    init = tuple((jnp.full((tq, 1), NEG_BIG, F32), jnp.zeros((tq, 1), F32), jnp.zeros((tq, HEAD_DIM), F32))
                 for _ in range(GROUP_HEADS))
    carry = lax.fori_loop(0, qi, functools.partial(step, masked=False), init)
    carry = step(qi, carry, True)
    outs = []
    for h in range(GROUP_HEADS):
        m, l, acc = carry[h]
        outs.append(_rms(acc / l, hg_ref[:, h * HEAD_DIM:(h + 1) * HEAD_DIM]))
    o_ref[0] = jnp.concatenate(outs, axis=1)


def _fox_prompt(q, kt, vt, small, frow, hg):
    nb, t, w = q.shape
    tq = min(ATTN_Q_TILE, t)
    return pl.pallas_call(
        functools.partial(_fox_kernel, tq=tq),
        out_shape=jax.ShapeDtypeStruct((nb, t, w), F32),
        grid=(nb, t // tq),
        in_specs=[pl.BlockSpec((1, tq, w), lambda b, i: (b, i, 0)),
                  pl.BlockSpec((1, w, t), lambda b, i: (b, 0, 0)),
                  pl.BlockSpec((1, w, t), lambda b, i: (b, 0, 0)),
                  pl.BlockSpec((1, tq, SMALL_W), lambda b, i: (b, i, 0)),
                  pl.BlockSpec((1, GROUP_HEADS, t), lambda b, i: (b, 0, 0)),
                  pl.BlockSpec((1, w), lambda b, i: (0, 0))],
        out_specs=pl.BlockSpec((1, tq, w), lambda b, i: (b, i, 0)),
        scratch_shapes=[pltpu.VMEM((w, t), BF16), pltpu.VMEM((w, t), BF16)],
        compiler_params=_cparams(("arbitrary", "arbitrary")),
        name="fox_prompt",
    )(q, kt, vt, small, frow, hg)


def _sb_kernel(q_ref, kt_ref, vt_ref, hg_ref, o_ref, ks_ref, vs_ref, *, tq, tk):
    qi = pl.program_id(1)

    @pl.when(qi == 0)
    def _():
        ks_ref[...] = kt_ref[0].astype(BF16)
        vs_ref[...] = vt_ref[0].astype(BF16)

    q_all = q_ref[0]
    qhs = [(q_all[:, h * HEAD_DIM:(h + 1) * HEAD_DIM] * QK_SCALE).astype(BF16) for h in range(GROUP_HEADS)]
    later = _tri01(tk, strict=True, lower=True)
    per_q = tq // tk

    def step(j, carry, masked):
        start = pl.multiple_of(j * tk, tk)
        if masked:
            before = (_iota((tq, tk), 1) + j * tk) < (_iota((tq, tk), 0) + qi * tq)
        heads = range(GROUP_HEADS)
        rows = [slice(h * HEAD_DIM, (h + 1) * HEAD_DIM) for h in heads]
        zs = [_dot(qhs[h], ks_ref[rows[h], pl.ds(start, tk)]) for h in heads]
        log_keeps = []
        for h in heads:
            z = zs[h]
            log_keep = -(jnp.maximum(z, 0.0) + _softplus_parts(z))
            if masked:
                log_keep = jnp.where(before, log_keep, 0.0)
            log_keeps.append(log_keep)
        afters = [_dot_exact_r(_dot, log_keeps[h], later, pieces=2) for h in heads]
        wgts = []
        for h in heads:
            wgt = jnp.exp(log_keeps[h] + zs[h] + (afters[h] + carry[h][0]))
            if masked:
                wgt = jnp.where(before, wgt, 0.0)
            wgts.append(wgt.astype(BF16))
        pv = [_dot_nt(wgts[h], vs_ref[rows[h], pl.ds(start, tk)]) for h in heads]
        return tuple((carry[h][0] + jnp.sum(log_keeps[h], axis=1, keepdims=True), carry[h][1] + pv[h]) for h in heads)

    carry = tuple((jnp.zeros((tq, 1), F32), jnp.zeros((tq, HEAD_DIM), F32)) for _ in range(GROUP_HEADS))
    for d in range(per_q):
        carry = step(qi * per_q + (per_q - 1 - d), carry, True)
    n_before = qi * per_q
    carry = lax.fori_loop(0, n_before, lambda i, c: step(n_before - 1 - i, c, False), carry)
    outs = [_rms(carry[h][1], hg_ref[:, h * HEAD_DIM:(h + 1) * HEAD_DIM]) for h in range(GROUP_HEADS)]
    o_ref[0] = jnp.concatenate(outs, axis=1)


def _sb_prompt(q, kt, vt, hg):
    nb, t, w = q.shape
    tq = min(ATTN_Q_TILE, t)
    tk = min(SB_KEY_TILE, t)
    return pl.pallas_call(
        functools.partial(_sb_kernel, tq=tq, tk=tk),
        out_shape=jax.ShapeDtypeStruct((nb, t, w), F32),
        grid=(nb, t // tq),
        in_specs=[pl.BlockSpec((1, tq, w), lambda b, i: (b, i, 0)),
                  pl.BlockSpec((1, w, t), lambda b, i: (b, 0, 0)),
                  pl.BlockSpec((1, w, t), lambda b, i: (b, 0, 0)),
                  pl.BlockSpec((1, w), lambda b, i: (0, 0))],
        out_specs=pl.BlockSpec((1, tq, w), lambda b, i: (b, i, 0)),
        scratch_shapes=[pltpu.VMEM((w, t), BF16), pltpu.VMEM((w, t), BF16)],
        compiler_params=_cparams(("arbitrary", "arbitrary")),
        name="sb_prompt",
    )(q, kt, vt, hg)


def _unit_lower_inverse(m_low, c):
    r, cc = _iota((c, c), 0), _iota((c, c), 1)
    eye = jnp.where(r == cc, 1.0, 0.0)
    ps = [eye - m for m in m_low]
    mps = list(m_low)
    for _ in range(int(math.log2(c)) - 1):
        mpb = [mp.astype(BF16) for mp in mps]
        mps = [_dot(b, b) for b in mpb]
        ps = [p + _dot(p.astype(BF16), mp.astype(BF16)) for p, mp in zip(ps, mps)]
    resid = [(eye - p) - _dot3(_dot, m, p) for m, p in zip(m_low, ps)]
    return [p + _dot(p.astype(BF16), rs.astype(BF16)) for p, rs in zip(ps, resid)]


def _gdn_kernel(x_ref, sm_ref, gate_ref, cb_ref, s0_ref, cw_ref, hg_ref,
                o_ref, sn_ref, cbn_ref, tail_ref, st_ref, *, c, n_valid):
    ci = pl.program_id(1)
    n_chunks = pl.num_programs(1)
    n_seq = x_ref.shape[0]

    @pl.when(ci == 0)
    def _():
        tail_ref[...] = cb_ref[...]
        st_ref[...] = s0_ref[...]

    rowi = _iota((c, GDN_CONV_DIM), 0)
    lane = _iota((c, SMALL_W), 1)
    cum_rows = _tri01(c, strict=False, lower=True)
    sel = jnp.where(_iota((16, SMALL_W), 1) == _iota((16, SMALL_W), 0) + SM_G, 1.0, 0.0).astype(BF16)
    eye_d = jnp.where(_iota((HEAD_DIM, HEAD_DIM), 0) == _iota((HEAD_DIM, HEAD_DIM), 1), 1.0, 0.0).astype(BF16)
    ri, cj = _iota((c, c), 0), _iota((c, c), 1)
    incl = ri >= cj
    strict = ri > cj
    n_last = c if n_valid is None else n_valid
    if n_valid is not None:
        valid = _iota((c, 1), 0) < n_valid

    seqs = range(n_seq)
    ys, sms = [], []
    for b in seqs:
        x = x_ref[b]
        tail = tail_ref[b]
        y = x * cw_ref[CONV_WIDTH - 1:CONV_WIDTH, :]
        tail_tiled = jnp.concatenate([tail] * (c // 8), axis=0)
        for s in range(1, CONV_WIDTH):
            shifted = jnp.where(rowi < s, pltpu.roll(tail_tiled, s, axis=0), pltpu.roll(x, s, axis=0))
            y = y + shifted * cw_ref[CONV_WIDTH - 1 - s:CONV_WIDTH - s, :]
        ys.append(_silu(y))

        if n_last % 8 == 0:
            new_tail = x[n_last - 8:n_last, :]
        else:
            cat = jnp.concatenate([tail, x[0:8, :]], axis=0)
            new_tail = pltpu.roll(cat, 16 - n_last, axis=0)[0:8, :]
        tail_ref[b] = x[c - 8:c, :]
        cbn_ref[b] = new_tail

        sm = sm_ref[b]
        if n_valid is not None:
            sm = jnp.where(valid, sm, 0.0)
        sms.append(sm)

    g_ls = [jnp.where((lane >= SM_G) & (lane < SM_G + GROUP_HEADS), sm, 0.0) for sm in sms]
    gcums = [_dot_exact_l(_dot, cum_rows, g_l) for g_l in g_ls]
    grows = [_dot_exact_l(_dot_nt, sel, gcum) for gcum in gcums]

    ones_head = _group_ones(GROUP_WIDTH, HEAD_DIM)
    q_alls = [y[:, 0:GROUP_WIDTH] for y in ys]
    k_alls = [y[:, GROUP_WIDTH:2 * GROUP_WIDTH] for y in ys]
    q_ss = [_dot_exact_r(_dot, q * q, ones_head) for q in q_alls]
    k_ss = [_dot_exact_r(_dot, k * k, ones_head) for k in k_alls]
    q_alls = [q * lax.rsqrt(ss + EPS) * QK_SCALE for q, ss in zip(q_alls, q_ss)]
    k_alls = [k * lax.rsqrt(ss + EPS) for k, ss in zip(k_alls, k_ss)]

    pairs = [(b, h) for b in seqs for h in range(GROUP_HEADS)]
    qs, ks, kbs, gcs, egcs, decays, vbs = [], [], [], [], [], [], []
    for b, h in pairs:
        lo = h * HEAD_DIM
        y, sm = ys[b], sms[b]
        qh = q_alls[b][:, lo:lo + HEAD_DIM]
        kh = k_alls[b][:, lo:lo + HEAD_DIM]
        vh = y[:, 2 * GROUP_WIDTH + lo:2 * GROUP_WIDTH + lo + HEAD_DIM]
        if n_valid is not None:
            qh = jnp.where(valid, qh, 0.0)
            kh = jnp.where(valid, kh, 0.0)
            vh = jnp.where(valid, vh, 0.0)
        gc = gcums[b][:, SM_G + h:SM_G + h + 1]
        gr = grows[b][h:h + 1, :]
        bh = sm[:, SM_BETA + h:SM_BETA + h + 1]
        kb = kh * bh
        egc = jnp.exp(gc)
        qs.append(qh)
        ks.append(kh)
        kbs.append(kb)
        gcs.append(gc)
        egcs.append(egc)
        decays.append(jnp.exp(jnp.where(incl, gc - gr, NEG_BIG)))
        vbs.append(vh * bh)

    khbs = [kh.astype(BF16) for kh in ks]
    m_lows = [jnp.where(strict, _dot_nt(kb.astype(BF16), khb) * dec, 0.0) for kb, khb, dec in zip(kbs, khbs, decays)]
    a_qks = [(_dot_nt(qh.astype(BF16), khb) * dec).astype(BF16) for qh, khb, dec in zip(qs, khbs, decays)]
    g_lasts = [gc[c - 1:c, :] for gc in gcs]
    kdec_ts = [_dot_nt(eye_d, (kh * jnp.exp(gl - gc)).astype(BF16)).astype(BF16)
               for kh, gl, gc in zip(ks, g_lasts, gcs)]
    invs = _unit_lower_inverse(m_lows, c)
    us = [_dot3(_dot, inv, vb) for inv, vb in zip(invs, vbs)]
    ws = [_dot3(_dot, inv, kb * egc) for inv, kb, egc in zip(invs, kbs, egcs)]
    states = [st_ref[b, h] for b, h in pairs]
    sbs = [s.astype(BF16) for s in states]
    o_states = [_dot((qh * egc).astype(BF16), sb) for qh, egc, sb in zip(qs, egcs, sbs)]
    v_news = [u - _dot(w.astype(BF16), sb) for u, w, sb in zip(us, ws, sbs)]
    vnbs = [v.astype(BF16) for v in v_news]
    o_locals = [_dot(a, vnb) for a, vnb in zip(a_qks, vnbs)]
    s_adds = [_dot(kt, vnb) for kt, vnb in zip(kdec_ts, vnbs)]
    for i, (b, h) in enumerate(pairs):
        st_ref[b, h] = states[i] * jnp.exp(g_lasts[i]) + s_adds[i]
    for b in seqs:
        gate = gate_ref[b]
        outs = []
        for h in range(GROUP_HEADS):
            i, lo = b * GROUP_HEADS + h, h * HEAD_DIM
            o = _rms(o_states[i] + o_locals[i], hg_ref[:, lo:lo + HEAD_DIM]) * _silu(gate[:, lo:lo + HEAD_DIM])
            outs.append(o)
        o_ref[b] = jnp.concatenate(outs, axis=1)

    @pl.when(ci == n_chunks - 1)
    def _():
        sn_ref[...] = st_ref[...]


def _gated_deltanet(gqkv, small, ggate, conv_buf8, s0, conv_w, hg, *, c, n_valid):
    nb, t, _ = gqkv.shape
    g = math.gcd(GDN_BATCH, nb)
    hd, gh = HEAD_DIM, GROUP_HEADS
    return pl.pallas_call(
        functools.partial(_gdn_kernel, c=c, n_valid=n_valid),
        out_shape=[jax.ShapeDtypeStruct((nb, t, GROUP_WIDTH), F32),
                   jax.ShapeDtypeStruct((nb, gh, hd, hd), F32),
                   jax.ShapeDtypeStruct((nb, 8, GDN_CONV_DIM), F32)],
        grid=(nb // g, t // c),
        in_specs=[pl.BlockSpec((g, c, GDN_CONV_DIM), lambda b, i: (b, i, 0)),
                  pl.BlockSpec((g, c, SMALL_W), lambda b, i: (b, i, 0)),
                  pl.BlockSpec((g, c, GROUP_WIDTH), lambda b, i: (b, i, 0)),
                  pl.BlockSpec((g, 8, GDN_CONV_DIM), lambda b, i: (b, 0, 0)),
                  pl.BlockSpec((g, gh, hd, hd), lambda b, i: (b, 0, 0, 0)),
                  pl.BlockSpec((CONV_WIDTH, GDN_CONV_DIM), lambda b, i: (0, 0)),
                  pl.BlockSpec((1, GROUP_WIDTH), lambda b, i: (0, 0))],
        out_specs=[pl.BlockSpec((g, c, GROUP_WIDTH), lambda b, i: (b, i, 0)),
                   pl.BlockSpec((g, gh, hd, hd), lambda b, i: (b, 0, 0, 0)),
                   pl.BlockSpec((g, 8, GDN_CONV_DIM), lambda b, i: (b, 0, 0))],
        scratch_shapes=[pltpu.VMEM((g, 8, GDN_CONV_DIM), F32),
                        pltpu.VMEM((g, gh, hd, hd), F32)],
        compiler_params=_cparams(("arbitrary", "arbitrary")),
        name="gated_deltanet",
    )(gqkv, small, ggate, conv_buf8, s0, conv_w, hg)


def _gmlp_kernel(uv_ref, vg_ref, w_ref, bias_ref, hg_ref, o_ref, v_ref):
    x = uv_ref[0]
    gelu = x * (0.5 * (1.0 + jnp.tanh(math.sqrt(2.0 / math.pi) * (x + 0.044715 * (x * x * x)))))
    u = gelu[:, :GROUP_WIDTH]
    v = _rms(gelu[:, GROUP_WIDTH:], vg_ref[...])
    v_ref[0] = v
    vb = v.astype(BF16)
    n = w_ref.shape[-1]
    chunks = x.shape[0] // n
    v_wide = jnp.concatenate([vb[i * n:(i + 1) * n, :] for i in range(chunks)], axis=1)
    tri = _iota((n, n), 0) >= _iota((n, n), 1)
    lane_group = (_iota((n, chunks * GROUP_WIDTH), 1) % GROUP_WIDTH) // HEAD_DIM
    mixed_wide = None
    for g in range(GROUP_HEADS):
        wg = jnp.where(tri, w_ref[g], 0.0).astype(BF16)
        part = jnp.where(lane_group == g, _dot(wg, v_wide), 0.0)
        mixed_wide = part if mixed_wide is None else mixed_wide + part
    mixed = jnp.concatenate([mixed_wide[:, i * GROUP_WIDTH:(i + 1) * GROUP_WIDTH] + bias_ref[...]
                             for i in range(chunks)], axis=0)
    o_ref[0] = _head_norm_lanes(u * mixed, hg_ref[...])


def _gmlp(guv, vg, w_s, bias_full, hg):
    nb, t, _ = guv.shape
    n = w_s.shape[-1]
    rows = n * math.gcd(GMLP_CHUNKS, t // n)
    return pl.pallas_call(
        _gmlp_kernel,
        out_shape=[jax.ShapeDtypeStruct((nb, t, GROUP_WIDTH), F32),
                   jax.ShapeDtypeStruct((nb, t, GROUP_WIDTH), F32)],
        grid=(nb, t // rows),
        in_specs=[pl.BlockSpec((1, rows, 2 * GROUP_WIDTH), lambda b, i: (b, i, 0)),
                  pl.BlockSpec((1, GROUP_WIDTH), lambda b, i: (0, 0)),
                  pl.BlockSpec((GROUP_HEADS, n, n), lambda b, i: (0, 0, 0)),
                  pl.BlockSpec((n, GROUP_WIDTH), lambda b, i: (0, 0)),
                  pl.BlockSpec((1, GROUP_WIDTH), lambda b, i: (0, 0))],
        out_specs=[pl.BlockSpec((1, rows, GROUP_WIDTH), lambda b, i: (b, i, 0)),
                   pl.BlockSpec((1, rows, GROUP_WIDTH), lambda b, i: (b, i, 0))],
        compiler_params=_cparams(("arbitrary", "arbitrary")),
        name="gmlp_gating",
    )(guv, vg, w_s, bias_full, hg)


def _post_kernel(of_ref, os_ref, og_ref, om_ref, x_ref, g1_ref, sh2_ref, sc2_ref, g2_ref, ng_ref,
                 wo_ref, w1_ref, w2_ref, y_ref, x1_ref, h_ref, acc_ref):
    f = pl.program_id(2)

    @pl.when(f == 0)
    def _():
        m = _dot(of_ref[0].astype(BF16), wo_ref[0 * GROUP_WIDTH:1 * GROUP_WIDTH, :])
        m = m + _dot(os_ref[0].astype(BF16), wo_ref[1 * GROUP_WIDTH:2 * GROUP_WIDTH, :])
        m = m + _dot(og_ref[0].astype(BF16), wo_ref[2 * GROUP_WIDTH:3 * GROUP_WIDTH, :])
        m = m + _dot(om_ref[0].astype(BF16), wo_ref[3 * GROUP_WIDTH:4 * GROUP_WIDTH, :])
        x1 = x_ref[0] + g1_ref[0] * _rms(m, ng_ref[1:2, :])
        x1_ref[...] = x1
        h = _rms(x1, ng_ref[2:3, :]) * (1.0 + sc2_ref[0]) + sh2_ref[0]
        h_ref[...] = h.astype(BF16)
        acc_ref[...] = jnp.zeros_like(acc_ref)

    a = jnp.maximum(_dot(h_ref[...], w1_ref[...]), 0.0)
    acc_ref[...] += _dot((a * a).astype(BF16), w2_ref[...])

    @pl.when(f == pl.num_programs(2) - 1)
    def _():
        y_ref[0] = x1_ref[...] + g2_ref[0] * _rms(acc_ref[...], ng_ref[3:4, :])


def _post(o_fox, o_sb, o_gdn, o_gmlp, x, gate1, shift2, scale2, gate2, norm_g, w_out, w1, w2, *, layer):
    nb, t, d = x.shape
    dff = w1.shape[-1]
    tm = min(MLP_ROW_TILE, t)
    tf = min(512, dff)
    tmod = gate1.shape[1]
    bm = tm if tmod == t else 1
    mod_spec = pl.BlockSpec((1, bm, d), (lambda b, i, f: (b, i, 0)) if tmod == t else (lambda b, i, f: (b, 0, 0)))
    o_spec = pl.BlockSpec((1, tm, GROUP_WIDTH), lambda b, i, f: (b, i, 0))
    return pl.pallas_call(
        _post_kernel,
        out_shape=jax.ShapeDtypeStruct((nb, t, d), F32),
        grid=(nb, t // tm, dff // tf),
        in_specs=[o_spec, o_spec, o_spec, o_spec,
                  pl.BlockSpec((1, tm, d), lambda b, i, f: (b, i, 0)),
                  mod_spec, mod_spec, mod_spec, mod_spec,
                  pl.BlockSpec((4, d), lambda b, i, f: (0, 0)),
                  pl.BlockSpec((None, d, d), lambda b, i, f: (layer, 0, 0)),
                  pl.BlockSpec((None, d, tf), lambda b, i, f: (layer, 0, f)),
                  pl.BlockSpec((None, tf, d), lambda b, i, f: (layer, f, 0))],
        out_specs=pl.BlockSpec((1, tm, d), lambda b, i, f: (b, i, 0)),
        scratch_shapes=[pltpu.VMEM((tm, d), F32), pltpu.VMEM((tm, d), BF16), pltpu.VMEM((tm, d), F32)],
        compiler_params=_cparams(("arbitrary", "arbitrary", "arbitrary")),
        name="outproj_mlp",
    )(o_fox, o_sb, o_gdn, o_gmlp, x, gate1, shift2, scale2, gate2, norm_g, w_out, w1, w2)


def _bias_rows(n_pages):
    return GROUP_HEADS * (n_pages + 2)


def _dec_bias_kernel(pt_ref, cache_ref, new_ref, later_ref, o_ref, buf_ref, sem, *, layer, n_pages):
    nb = o_ref.shape[0]
    rows = _bias_rows(n_pages)
    total = nb * n_pages

    def page_copy(i):
        b = i // n_pages
        p = i - b * n_pages
        dst = pl.multiple_of(b * rows + GROUP_HEADS * p, GROUP_HEADS)
        return pltpu.make_async_copy(cache_ref.at[layer, pt_ref[b, p]],
                                     buf_ref.at[pl.ds(dst, GROUP_HEADS), :], sem)

    def start(i, c):
        page_copy(i).start()
        return c

    def wait(i, c):
        page_copy(i).wait()
        return c

    lax.fori_loop(0, total, start, 0)
    lax.fori_loop(0, total, wait, 0)

    after = _tri01(PAGE_SIZE, strict=True, lower=True)
    later = later_ref[...]
    for b in range(nb):
        buf_ref[b * rows + GROUP_HEADS * n_pages:(b + 1) * rows, :] = new_ref[b]
        x = buf_ref[b * rows:(b + 1) * rows, :]
        within = _dot_exact_r(_dot, x, after)
        tot = jnp.broadcast_to(jnp.sum(x, axis=1, keepdims=True), x.shape)
        o_ref[b] = within + _dot_exact_l(_dot, later, tot)


def _dec_bias(page_table, cache_logf_t, new_rows, later, *, layer):
    nb, n_pages = page_table.shape
    rows = _bias_rows(n_pages)
    return pl.pallas_call(
        functools.partial(_dec_bias_kernel, layer=layer, n_pages=n_pages),
        out_shape=jax.ShapeDtypeStruct((nb, rows, PAGE_SIZE), F32),
        grid_spec=pltpu.PrefetchScalarGridSpec(
            num_scalar_prefetch=1,
            grid=(1,),
            in_specs=[pl.BlockSpec(memory_space=pl.ANY),
                      pl.BlockSpec((nb, 2 * GROUP_HEADS, PAGE_SIZE), lambda i, pt: (0, 0, 0)),
                      pl.BlockSpec((rows, rows), lambda i, pt: (0, 0))],
            out_specs=pl.BlockSpec((nb, rows, PAGE_SIZE), lambda i, pt: (0, 0, 0)),
            scratch_shapes=[pltpu.VMEM((nb * rows, PAGE_SIZE), F32), pltpu.SemaphoreType.DMA(())]),
        compiler_params=_cparams(("arbitrary",)),
        name="decode_forget_bias",
    )(page_table, cache_logf_t, new_rows, later)


def _rows_by_head(per_head, rows):
    per = rows // GROUP_HEADS
    rg = _iota((rows, PAGE_SIZE), 0) // per
    out = jnp.broadcast_to(per_head[0:1, :], (rows, PAGE_SIZE))
    for h in range(1, GROUP_HEADS):
        out = jnp.where(rg == h, jnp.broadcast_to(per_head[h:h + 1, :], (rows, PAGE_SIZE)), out)
    return out


def _dec_attn_kernel(pt_ref, *refs, pp, n_pages, n_q):
    del pt_ref
    caches = refs[:4 * pp]
    (qf_ref, qs_ref, nfk_ref, nfv_ref, nsk_ref, nsv_ref, bias_ref, hgf_ref, hgs_ref,
     of_ref, os_ref, qbf_ref, qbs_ref, m_ref, l_ref, accf_ref, run_ref, accs_ref, ft_ref) = refs[4 * pp:]
    g = pl.program_id(1)
    rows = GROUP_HEADS * n_q
    row_head = _iota((rows, GROUP_WIDTH), 0) // n_q
    lane_head = _iota((rows, GROUP_WIDTH), 1) // HEAD_DIM
    own = row_head == lane_head
    later = _tri01(PAGE_SIZE, strict=True, lower=True)

    def attend(kfs, vfs, biases, fox_mask, kss, vss, sb_mask, transposed):
        qk, pv = (_dot, _dot_nt) if transposed else (_dot_nt, _dot)
        n = len(kfs)
        qf, qs = qbf_ref[...], qbs_ref[...]
        s_parts = [qk(qf, kf) for kf in kfs]
        zs = [qk(qs, ks) for ks in kss]

        s = jnp.concatenate(s_parts, axis=1) + (jnp.concatenate(biases, axis=1) - ft_ref[...])
        if fox_mask is not None:
            s = jnp.where(fox_mask, s, NEG_BIG)
        m = m_ref[...]
        m_new = jnp.maximum(m, jnp.max(s, axis=1, keepdims=True))
        alpha = jnp.exp(m - m_new)
        p = jnp.exp(s - m_new)
        l_ref[...] = alpha * l_ref[...] + jnp.sum(p, axis=1, keepdims=True)
        m_ref[...] = m_new

        log_keeps = []
        for z in zs:
            log_keep = -(jnp.maximum(z, 0.0) + _softplus_parts(z))
            if sb_mask is not None:
                log_keep = jnp.where(sb_mask, log_keep, 0.0)
            log_keeps.append(log_keep)
        afters = [_dot_exact_r(_dot, lk, later, pieces=2) for lk in log_keeps]
        fox_parts = [pv(p[:, r * PAGE_SIZE:(r + 1) * PAGE_SIZE].astype(BF16), vfs[r]) for r in range(n)]

        run = run_ref[...]
        wgts = []
        for r in range(n):
            wgt = jnp.exp(log_keeps[r] + zs[r] + (afters[r] + run))
            if sb_mask is not None:
                wgt = jnp.where(sb_mask, wgt, 0.0)
            wgts.append(wgt.astype(BF16))
            run = run + jnp.sum(log_keeps[r], axis=1, keepdims=True)
        run_ref[...] = run
        sb_parts = [pv(wgts[r], vss[r]) for r in range(n)]

        accf_ref[...] = alpha * accf_ref[...] + functools.reduce(lambda a, b: a + b, fox_parts)
        accs_ref[...] = accs_ref[...] + functools.reduce(lambda a, b: a + b, sb_parts)

    @pl.when(g == 0)
    def _():
        qbf_ref[...] = jnp.where(own, qf_ref[0] * QK_SCALE, 0.0).astype(BF16)
        qbs_ref[...] = jnp.where(own, qs_ref[0] * QK_SCALE, 0.0).astype(BF16)
        m_ref[...] = jnp.full_like(m_ref, NEG_BIG)
        l_ref[...] = jnp.zeros_like(l_ref)
        accf_ref[...] = jnp.zeros_like(accf_ref)
        accs_ref[...] = jnp.zeros_like(accs_ref)
        run_ref[...] = jnp.zeros_like(run_ref)
        bias16 = _rows_by_head(bias_ref[0, GROUP_HEADS * n_pages:GROUP_HEADS * (n_pages + 1), :], rows)
        qpos = _iota((rows, PAGE_SIZE), 0) % n_q
        kpos = _iota((rows, PAGE_SIZE), 1)
        ft_ref[...] = jnp.sum(jnp.where(kpos == qpos, bias16, 0.0), axis=1, keepdims=True)
        pad = jnp.zeros((PAGE_SIZE - 8, GROUP_WIDTH), F32)

        def page_of(r):
            return jnp.concatenate([r[0], pad], axis=0).astype(BF16)

        attend([page_of(nfk_ref)], [page_of(nfv_ref)], [bias16], (kpos <= qpos) & (kpos < n_q),
               [page_of(nsk_ref)], [page_of(nsv_ref)], kpos < qpos, False)

    first = pl.multiple_of(GROUP_HEADS * (n_pages - (g + 1) * pp), GROUP_HEADS * pp)
    slab = bias_ref[0, pl.ds(first, GROUP_HEADS * pp), :]
    biases = [_rows_by_head(slab[GROUP_HEADS * (pp - 1 - r):GROUP_HEADS * (pp - r), :], rows) for r in range(pp)]
    page = lambda r, i: caches[4 * r + i][0, 0].astype(BF16)
    attend([page(r, 0) for r in range(pp)], [page(r, 1) for r in range(pp)], biases, None,
           [page(r, 2) for r in range(pp)], [page(r, 3) for r in range(pp)], None, True)

    @pl.when(g == pl.num_programs(1) - 1)
    def _():
        def fold(acc):
            a = jnp.where(own, acc, 0.0)
            out = a[0:n_q, :]
            for h in range(1, GROUP_HEADS):
                out = out + a[h * n_q:(h + 1) * n_q, :]
            return out

        of_ref[0] = _head_norm_lanes(fold(accf_ref[...] / l_ref[...]), hgf_ref[...])
        os_ref[0] = _head_norm_lanes(fold(accs_ref[...]), hgs_ref[...])


def _dec_attention(page_table, cfk, cfv, csk, csv, qf16, qs16, nfk, nfv, nsk, nsv, bias, hgf, hgs, *, layer, n_q):
    nb, n_pages = page_table.shape
    pp = math.gcd(DEC_PAGES, n_pages)
    assert pp % 2 == 0
    n_groups = n_pages // pp
    rows = GROUP_HEADS * n_q

    def cache_spec(r):
        def imap(b, g, pt):
            page = n_pages - 1 - (g * pp + r)
            return (layer, pt[b, page], 0, 0)
        return pl.BlockSpec((1, 1, GROUP_WIDTH, PAGE_SIZE), imap)

    cache_specs, cache_args = [], []
    for r in range(pp):
        for arr in (cfk, cfv, csk, csv):
            cache_specs.append(cache_spec(r))
            cache_args.append(arr)
    per_b3 = lambda b, g, pt: (b, 0, 0)
    const2 = lambda b, g, pt: (0, 0)
    return pl.pallas_call(
        functools.partial(_dec_attn_kernel, pp=pp, n_pages=n_pages, n_q=n_q),
        out_shape=[jax.ShapeDtypeStruct((nb, n_q, GROUP_WIDTH), F32)] * 2,
        grid_spec=pltpu.PrefetchScalarGridSpec(
            num_scalar_prefetch=1,
            grid=(nb, n_groups),
            in_specs=cache_specs + [
                pl.BlockSpec((1, rows, GROUP_WIDTH), per_b3),
                pl.BlockSpec((1, rows, GROUP_WIDTH), per_b3),
                pl.BlockSpec((1, 8, GROUP_WIDTH), per_b3),
                pl.BlockSpec((1, 8, GROUP_WIDTH), per_b3),
                pl.BlockSpec((1, 8, GROUP_WIDTH), per_b3),
                pl.BlockSpec((1, 8, GROUP_WIDTH), per_b3),
                pl.BlockSpec((1, _bias_rows(n_pages), PAGE_SIZE), per_b3),
                pl.BlockSpec((1, GROUP_WIDTH), const2),
                pl.BlockSpec((1, GROUP_WIDTH), const2)],
            out_specs=[pl.BlockSpec((1, n_q, GROUP_WIDTH), per_b3)] * 2,
            scratch_shapes=[pltpu.VMEM((rows, GROUP_WIDTH), BF16), pltpu.VMEM((rows, GROUP_WIDTH), BF16),
                            pltpu.VMEM((rows, 1), F32), pltpu.VMEM((rows, 1), F32),
                            pltpu.VMEM((rows, GROUP_WIDTH), F32),
                            pltpu.VMEM((rows, 1), F32), pltpu.VMEM((rows, GROUP_WIDTH), F32),
                            pltpu.VMEM((rows, 1), F32)]),
        compiler_params=_cparams(("arbitrary", "arbitrary")),
        name="decode_attention",
    )(page_table, *cache_args, qf16, qs16, nfk, nfv, nsk, nsv, bias, hgf, hgs)


def _permute_w_in(w_in):
    gw, gh = GROUP_WIDTH, GROUP_HEADS
    o = 0
    fq = w_in[..., o:o + gw]; o += gw
    fk = w_in[..., o:o + gw]; o += gw
    fv = w_in[..., o:o + gw]; o += gw
    ff = w_in[..., o:o + gh]; o += gh
    sq = w_in[..., o:o + gw]; o += gw
    sk = w_in[..., o:o + gw]; o += gw
    sv = w_in[..., o:o + gw]; o += gw
    gqkv = w_in[..., o:o + GDN_CONV_DIM]; o += GDN_CONV_DIM
    ga = w_in[..., o:o + gh]; o += gh
    gb = w_in[..., o:o + gh]; o += gh
    ggate = w_in[..., o:o + gw]; o += gw
    guv = w_in[..., o:o + 2 * gw]
    pad = jnp.zeros(w_in.shape[:-1] + (SMALL_W - 4 * gh,), w_in.dtype)
    w = jnp.concatenate([fq, fk, fv, sq, sk, sv, gqkv, ggate, guv, ff, ga, gb, ff, pad], axis=-1).astype(BF16)
    wt = jnp.swapaxes(jnp.concatenate([fk, fv, sk, sv], axis=-1), -1, -2).astype(BF16)
    return w, wt


def _small_params(b_forget, a_log, dt_bias):
    z4 = jnp.zeros((GROUP_HEADS,), F32)
    zpad = jnp.zeros((SMALL_W - 4 * GROUP_HEADS,), F32)
    bias = jnp.concatenate([b_forget, dt_bias, z4, b_forget, zpad])
    alog = jnp.concatenate([z4, a_log, z4, z4, zpad])
    return jnp.zeros((8, SMALL_W), F32).at[0].set(bias).at[1].set(alog)


def _later_pages_matrix(n_pages):
    idx = jnp.arange(_bias_rows(n_pages))
    page, head = idx // GROUP_HEADS, idx % GROUP_HEADS
    return ((head[:, None] == head[None, :]) & (page[None, :] > page[:, None])).astype(BF16)


def _pages_transposed(cache):
    depth, n_phys = cache.shape[:2]
    return jnp.transpose(cache, (0, 1, 3, 4, 2)).reshape(depth, n_phys, GROUP_WIDTH, PAGE_SIZE)


def _heads_last(a_t, t):
    return jnp.transpose(a_t.reshape(a_t.shape[0], GROUP_HEADS, HEAD_DIM, t), (0, 3, 1, 2))


def kernel(x_prompt, x_sample, cache_fox_k, cache_fox_v, cache_fox_logf, cache_sb_k, cache_sb_v, state_gdn, state_gdn_conv, page_table, c_prompt, c_sample, ada_w, ada_b, norm_g, w_in, b_forget, gdn_conv_w, gdn_a_log, gdn_dt_bias, gmlp_v_norm_g, gmlp_w_s, gmlp_b_s, head_norm_g, w_out, w_ff1, w_ff2):
    depth = ada_w.shape[0]
    n_p, t_p, d = x_prompt.shape
    n_s, t_s, _ = x_sample.shape
    n_phys = cache_fox_k.shape[1]
    rows_s = n_s * t_s
    gw, gh = GROUP_WIDTH, GROUP_HEADS

    mods = _modulation(jnp.concatenate([c_prompt, c_sample], axis=0), ada_w, ada_b)
    mods_p = mods[:, :n_p].reshape(depth, n_p, 1, N_MOD, d)
    mods_s = jnp.repeat(mods[:, n_p:], t_s, axis=1).reshape(depth, 1, rows_s, N_MOD, d)

    n_pages = page_table.shape[1]
    cfk, cfv = _pages_transposed(cache_fox_k), _pages_transposed(cache_fox_v)
    csk, csv = _pages_transposed(cache_sb_k), _pages_transposed(cache_sb_v)
    clogf = jnp.swapaxes(cache_fox_logf, 2, 3)
    later = _later_pages_matrix(n_pages)

    zeros_cb = jnp.zeros((n_p, 8, GDN_CONV_DIM), F32)
    zeros_s0 = jnp.zeros((n_p, gh, HEAD_DIM, HEAD_DIM), F32)
    eye_s = jnp.eye(n_s, dtype=F32)

    w_all, wt_all = _permute_w_in(w_in)
    wo_all, w1_all, w2_all = w_out.astype(BF16), w_ff1.astype(BF16), w_ff2.astype(BF16)

    x_p = x_prompt
    x_s = x_sample.reshape(1, rows_s, d)
    outs_p, outs_s = [], []
    for l in range(depth):
        sp = _small_params(b_forget[l], gdn_a_log[l], gdn_dt_bias[l])
        hg = head_norm_g[l].reshape(4, 1, gw)
        ng = norm_g[l]
        bias_p = jnp.repeat(gmlp_b_s[l].T, HEAD_DIM, axis=1)
        vg = gmlp_v_norm_g[l].reshape(1, gw)

        mp = [mods_p[l, :, :, i] for i in range(N_MOD)]
        fq, fk, fv, sq, sk, sv, gqkv, ggate, guv, small = _in_projection(
            x_p, mp[0], mp[1], ng[0:1], w_all, wt_all, sp, layer=l, cumsum=True, kv_transposed=True)
        frow = jnp.swapaxes(small[:, :, SM_CUM:SM_CUM + gh], 1, 2)
        o_fox = _fox_prompt(fq, fk, fv, small, frow, hg[0])
        o_sb = _sb_prompt(sq, sk, sv, hg[1])
        o_gdn, s_new, cb_new = _gated_deltanet(gqkv, small, ggate, zeros_cb, zeros_s0, gdn_conv_w[l], hg[2],
                                               c=GDN_CHUNK, n_valid=None)
        o_gmlp, _ = _gmlp(guv, vg, gmlp_w_s[l], bias_p, hg[3])
        x_p = _post(o_fox, o_sb, o_gdn, o_gmlp, x_p, mp[2], mp[3], mp[4], mp[5], ng, wo_all, w1_all, w2_all, layer=l)
        outs_p.append((_heads_last(fk, t_p), _heads_last(fv, t_p), small[:, :, SM_LOGF:SM_LOGF + gh],
                       _heads_last(sk, t_p), _heads_last(sv, t_p), s_new, cb_new[:, 8 - (CONV_WIDTH - 1):]))

        ms = [mods_s[l, :, :, i] for i in range(N_MOD)]
        fq, fk, fv, sq, sk, sv, gqkv, ggate, guv, small = _in_projection(
            x_s, ms[0], ms[1], ng[0:1], w_all, wt_all, sp, layer=l, cumsum=False, kv_transposed=False)
        logf_new = small[0, :, SM_LOGF:SM_LOGF + gh]
        new_rows = jnp.pad(jnp.swapaxes(logf_new.reshape(n_s, t_s, gh), 1, 2),
                           ((0, 0), (0, gh), (0, PAGE_SIZE - t_s)))
        bias = _dec_bias(page_table, clogf, new_rows, later, layer=l)
        per_seq = lambda a: a.reshape(n_s, t_s, gw)
        pad8 = lambda a: jnp.pad(per_seq(a), ((0, 0), (0, 8 - t_s), (0, 0)))
        tile_q = lambda a: jnp.tile(per_seq(a), (1, gh, 1))
        o_fox, o_sb = _dec_attention(page_table, cfk, cfv, csk, csv, tile_q(fq), tile_q(sq),
                                     pad8(fk), pad8(fv), pad8(sk), pad8(sv), bias, hg[0], hg[1],
                                     layer=l, n_q=t_s)
        cs = GDN_CHUNK_SMALL
        padc = lambda a: jnp.pad(a.reshape(n_s, t_s, a.shape[-1]), ((0, 0), (0, cs - t_s), (0, 0)))
        cb8 = jnp.pad(state_gdn_conv[l], ((0, 0), (8 - (CONV_WIDTH - 1), 0), (0, 0)))
        o_gdn, s_new, cb_new = _gated_deltanet(padc(gqkv), padc(small), padc(ggate), cb8, state_gdn[l],
                                               gdn_conv_w[l], hg[2], c=cs, n_valid=t_s)
        o_gdn = o_gdn[:, :t_s].reshape(1, rows_s, gw)
        w_blk = jnp.einsum('ab,gij->gaibj', eye_s, gmlp_w_s[l][:, :t_s, :t_s]).reshape(gh, rows_s, rows_s)
        bias_s = jnp.tile(bias_p[:t_s], (n_s, 1))
        o_gmlp, v_rows = _gmlp(guv, vg, w_blk, bias_s, hg[3])
        x_s = _post(o_fox.reshape(1, rows_s, gw), o_sb.reshape(1, rows_s, gw), o_gdn, o_gmlp, x_s,
                    ms[2], ms[3], ms[4], ms[5], ng, wo_all, w1_all, w2_all, layer=l)
        outs_s.append((per_seq(fk[0]), per_seq(fv[0]), logf_new.reshape(n_s, t_s, gh), per_seq(sk[0]),
                       per_seq(sv[0]), s_new, cb_new[:, 8 - (CONV_WIDTH - 1):], per_seq(v_rows[0])))

    def stacked(states, i, shape=None):
        a = jnp.stack([s[i] for s in states])
        return a if shape is None else a.reshape(shape)

    hp = (depth, n_p, t_p, gh, HEAD_DIM)
    hs = (depth, n_s, t_s, gh, HEAD_DIM)
    return (x_p, x_s.reshape(n_s, t_s, d),
            stacked(outs_p, 0, hp), stacked(outs_p, 1, hp), stacked(outs_p, 2), stacked(outs_p, 3, hp),
            stacked(outs_p, 4, hp), stacked(outs_p, 5), stacked(outs_p, 6),
            stacked(outs_s, 0, hs), stacked(outs_s, 1, hs), stacked(outs_s, 2), stacked(outs_s, 3, hs),
            stacked(outs_s, 4, hs), stacked(outs_s, 5), stacked(outs_s, 6), stacked(outs_s, 7))
```

```python
import functools
import math

import jax
import jax.numpy as jnp
from jax import lax
from jax.experimental import pallas as pl
from jax.experimental.pallas import tpu as pltpu

F32 = jnp.float32
BF16 = jnp.bfloat16

HEAD_DIM = 64
GROUP_HEADS = 4
GROUP_WIDTH = GROUP_HEADS * HEAD_DIM
CONV_WIDTH = 4
GDN_CONV_DIM = 3 * GROUP_WIDTH
GDN_CHUNK = 64
GDN_CHUNK_SMALL = 16
MLP_CHUNK = 128
PAGE_SIZE = 128
N_MOD = 6
EPS = 1e-6
QK_SCALE = HEAD_DIM ** -0.5
NEG_BIG = -1e30
LOG2_E = 1.4426950408889634

SM_LOGF = 0
SM_G = 4
SM_BETA = 8
SM_CUM = 12
SMALL_W = 128

C_FQ, C_FK, C_FV, C_SQ, C_SK, C_SV = (i * GROUP_WIDTH for i in range(6))
C_GQKV = 6 * GROUP_WIDTH
C_GGATE = C_GQKV + GDN_CONV_DIM
C_GUV = C_GGATE + GROUP_WIDTH
C_SMALL = C_GUV + 2 * GROUP_WIDTH
IN_PAD_WIDTH = C_SMALL + SMALL_W

VMEM_LIMIT = 56 * 1024 * 1024

ATTN_Q_TILE = 512
FOX_KEY_TILE = 512
SB_KEY_TILE = 256
GDN_BATCH = 8
DEC_PAGES = 16
GMLP_CHUNKS = 4
MLP_ROW_TILE = 1024


def _cparams(sem):
    return pltpu.CompilerParams(dimension_semantics=sem, vmem_limit_bytes=VMEM_LIMIT)


def _dot(a, b):
    return jnp.dot(a, b, preferred_element_type=F32)


def _dot_nt(a, b):
    return lax.dot_general(a, b, (((1,), (1,)), ((), ())), preferred_element_type=F32)


def _split2(a):
    hi = a.astype(BF16)
    lo = (a - hi.astype(F32)).astype(BF16)
    return hi, lo


def _split3(a):
    hi = a.astype(BF16)
    r = a - hi.astype(F32)
    mid = r.astype(BF16)
    lo = (r - mid.astype(F32)).astype(BF16)
    return hi, mid, lo


def _dot3(dot, a, b):
    ah, al = _split2(a)
    bh, bl = _split2(b)
    return dot(ah, bh) + (dot(ah, bl) + dot(al, bh))


def _dot_exact_l(dot, a01, b, pieces=3):
    parts = _split3(b) if pieces == 3 else _split2(b)
    out = dot(a01, parts[0])
    for p in parts[1:]:
        out = out + dot(a01, p)
    return out


def _split2_trunc(a):
    hi = pltpu.bitcast(pltpu.bitcast(a, jnp.uint32) & jnp.uint32(0xFFFF0000), F32)
    return hi.astype(BF16), (a - hi).astype(BF16)


def _dot_exact_r(dot, a, b01, pieces=3):
    parts = _split3(a) if pieces == 3 else _split2_trunc(a)
    out = dot(parts[0], b01)
    for p in parts[1:]:
        out = out + dot(p, b01)
    return out


def _softplus_parts(x):
    return jnp.log(1.0 + jnp.exp2(jnp.abs(x) * (-LOG2_E)))


def _sigmoid(x):
    return 1.0 / (1.0 + jnp.exp(-x))


def _silu(x):
    return x * _sigmoid(x)


def _rms(x, g):
    return x * lax.rsqrt(jnp.mean(x * x, axis=-1, keepdims=True) + EPS) * g


def _iota(shape, dim):
    return lax.broadcasted_iota(jnp.int32, shape, dim)


def _tri01(n, strict, lower, dtype=BF16):
    r, c = _iota((n, n), 0), _iota((n, n), 1)
    if lower:
        m = (r > c) if strict else (r >= c)
    else:
        m = (r < c) if strict else (r <= c)
    return jnp.where(m, 1.0, 0.0).astype(dtype)


def _group_ones(n, group):
    r, c = _iota((n, n), 0), _iota((n, n), 1)
    return jnp.where((r // group) == (c // group), 1.0, 0.0).astype(BF16)


def _head_norm_lanes(x, gain):
    ones = _group_ones(x.shape[-1], HEAD_DIM)
    ms = _dot_exact_r(_dot, x * x, ones) * (1.0 / HEAD_DIM)
    return x * lax.rsqrt(ms + EPS) * gain


def _mod_kernel(c_ref, w_ref, b_ref, o_ref):
    a = _silu(c_ref[...]).astype(BF16)
    o_ref[0] = _dot(a, w_ref[0].astype(BF16)) + b_ref[0]


def _modulation(c_all, ada_w, ada_b):
    depth, d, n = ada_w.shape
    rows = c_all.shape[0]
    tn = 1536 if n % 1536 == 0 else n
    return pl.pallas_call(
        _mod_kernel,
        out_shape=jax.ShapeDtypeStruct((depth, rows, n), F32),
        grid=(depth, n // tn),
        in_specs=[pl.BlockSpec((rows, d), lambda l, j: (0, 0)),
                  pl.BlockSpec((1, d, tn), lambda l, j: (l, 0, j)),
                  pl.BlockSpec((1, 1, tn), lambda l, j: (l, 0, j))],
        out_specs=pl.BlockSpec((1, rows, tn), lambda l, j: (l, 0, j)),
        compiler_params=_cparams(("arbitrary", "arbitrary")),
        name="adaln_mod",
    )(c_all, ada_w, ada_b.reshape(depth, 1, n))


def _inproj_kernel(x_ref, sh_ref, sc_ref, g_ref, w_ref, wt_ref, sp_ref,
                   fq, fk, fv, sq, sk, sv, gqkv, ggate, guv, small, carry_ref, *, cumsum, kv_transposed):
    x = x_ref[0]
    h = _rms(x, g_ref[...]) * (1.0 + sc_ref[0]) + sh_ref[0]
    hb = h.astype(BF16)

    def seg(off, n):
        return _dot(hb, w_ref[:, off:off + n])

    def seg_t(i):
        return _dot_nt(wt_ref[i * GROUP_WIDTH:(i + 1) * GROUP_WIDTH, :], hb)

    fq[0] = seg(C_FQ, GROUP_WIDTH)
    sq[0] = seg(C_SQ, GROUP_WIDTH)
    if kv_transposed:
        fk[0] = seg_t(0)
        fv[0] = seg_t(1)
        sk[0] = seg_t(2)
        sv[0] = seg_t(3)
    else:
        fk[0] = seg(C_FK, GROUP_WIDTH)
        fv[0] = seg(C_FV, GROUP_WIDTH)
        sk[0] = seg(C_SK, GROUP_WIDTH)
        sv[0] = seg(C_SV, GROUP_WIDTH)
    gqkv[0] = seg(C_GQKV, GDN_CONV_DIM)
    ggate[0] = seg(C_GGATE, GROUP_WIDTH)
    guv[0] = seg(C_GUV, 2 * GROUP_WIDTH)

    z = seg(C_SMALL, SMALL_W)
    tm = z.shape[0]
    lane = _iota(z.shape, 1)
    zz = z + sp_ref[0:1, :]
    t = _softplus_parts(zz)
    logf = jnp.minimum(zz, 0.0) - t
    gdec = -jnp.exp(sp_ref[1:2, :]) * (jnp.maximum(zz, 0.0) + t)
    beta = _sigmoid(z)
    out = jnp.where(lane < SM_G, logf, jnp.where(lane < SM_BETA, gdec, jnp.where(lane < SM_CUM, beta, 0.0)))
    if cumsum:
        @pl.when(pl.program_id(1) == 0)
        def _():
            carry_ref[...] = jnp.zeros_like(carry_ref)

        lf = jnp.where((lane >= SM_CUM) & (lane < SM_CUM + GROUP_HEADS), logf, 0.0)
        cum = _dot_exact_l(_dot, _tri01(tm, strict=False, lower=True), lf) + carry_ref[0:1, :]
        carry_ref[0:1, :] = cum[tm - 1:tm, :]
        out = out + cum
    small[0] = out


def _in_projection(x, shift, scale, g, w, wt, sp, *, layer, cumsum, kv_transposed):
    nb, t, d = x.shape
    tm = min(512, t)
    tmod = shift.shape[1]
    bm = tm if tmod == t else 1
    mod_spec = pl.BlockSpec((1, bm, d), (lambda b, i: (b, i, 0)) if tmod == t else (lambda b, i: (b, 0, 0)))
    gw = GROUP_WIDTH
    row = lambda wd: (jax.ShapeDtypeStruct((nb, t, wd), F32), pl.BlockSpec((1, tm, wd), lambda b, i: (b, i, 0)))
    col = (jax.ShapeDtypeStruct((nb, gw, t), F32), pl.BlockSpec((1, gw, tm), lambda b, i: (b, 0, i)))
    kv = col if kv_transposed else row(gw)
    outs = [row(gw), kv, kv, row(gw), kv, kv, row(GDN_CONV_DIM), row(gw), row(2 * gw), row(SMALL_W)]
    return pl.pallas_call(
        functools.partial(_inproj_kernel, cumsum=cumsum, kv_transposed=kv_transposed),
        out_shape=[o[0] for o in outs],
        grid=(nb, t // tm),
        in_specs=[pl.BlockSpec((1, tm, d), lambda b, i: (b, i, 0)),
                  mod_spec, mod_spec,
                  pl.BlockSpec((1, d), lambda b, i: (0, 0)),
                  pl.BlockSpec((None, d, IN_PAD_WIDTH), lambda b, i: (layer, 0, 0)),
                  pl.BlockSpec((None, 4 * gw, d), lambda b, i: (layer, 0, 0)),
                  pl.BlockSpec((8, SMALL_W), lambda b, i: (0, 0))],
        out_specs=[o[1] for o in outs],
        scratch_shapes=[pltpu.VMEM((8, SMALL_W), F32)],
        compiler_params=_cparams(("arbitrary", "arbitrary")),
        name="in_projection",
    )(x, shift, scale, g, w, wt, sp)


def _fox_kernel(q_ref, kt_ref, vt_ref, smq_ref, frow_ref, hg_ref, o_ref, ks_ref, vs_ref, *, tq, tk):
    qi = pl.program_id(1)

    @pl.when(qi == 0)
    def _():
        ks_ref[...] = kt_ref[0].astype(BF16)
        vs_ref[...] = vt_ref[0].astype(BF16)

    q_all = q_ref[0]
    sm = smq_ref[0]
    qhs = [(q_all[:, h * HEAD_DIM:(h + 1) * HEAD_DIM] * QK_SCALE).astype(BF16) for h in range(GROUP_HEADS)]
    f_qs = [sm[:, SM_CUM + h:SM_CUM + h + 1] for h in range(GROUP_HEADS)]

    def step(j, carry, masked):
        start = pl.multiple_of(j * tk, tk)
        heads = range(GROUP_HEADS)
        rows = [slice(h * HEAD_DIM, (h + 1) * HEAD_DIM) for h in heads]
        qk = [_dot(qhs[h], ks_ref[rows[h], pl.ds(start, tk)]) for h in heads]
        if masked:
            visible = (_iota((tq, tk), 1) + j * tk) <= (_iota((tq, tk), 0) + qi * tq)
        stats, ps = [], []
        for h in heads:
            m, l, _ = carry[h]
            f_k = frow_ref[0, h:h + 1, pl.ds(start, tk)]
            s = qk[h] + (f_qs[h] - f_k)
            if masked:
                s = jnp.where(visible, s, NEG_BIG)
            m_new = jnp.maximum(m, jnp.max(s, axis=1, keepdims=True))
            alpha = jnp.exp(m - m_new)
            p = jnp.exp(s - m_new)
            stats.append((m_new, alpha, alpha * l + jnp.sum(p, axis=1, keepdims=True)))
            ps.append(p.astype(BF16))
        pv = [_dot_nt(ps[h], vs_ref[rows[h], pl.ds(start, tk)]) for h in heads]
        return tuple((stats[h][0], stats[h][2], stats[h][1] * carry[h][2] + pv[h]) for h in heads)

    per_q = tq // tk
    init = tuple((jnp.full((tq, 1), NEG_BIG, F32), jnp.zeros((tq, 1), F32), jnp.zeros((tq, HEAD_DIM), F32))
                 for _ in range(GROUP_HEADS))
    carry = lax.fori_loop(0, qi * per_q, functools.partial(step, masked=False), init)
    for d in range(per_q):
        carry = step(qi * per_q + d, carry, True)
    outs = []
    for h in range(GROUP_HEADS):
        m, l, acc = carry[h]
        outs.append(_rms(acc / l, hg_ref[:, h * HEAD_DIM:(h + 1) * HEAD_DIM]))
    o_ref[0] = jnp.concatenate(outs, axis=1)


def _fox_prompt(q, kt, vt, small, frow, hg):
    nb, t, w = q.shape
    tq = min(ATTN_Q_TILE, t)
    tk = min(FOX_KEY_TILE, t)
    return pl.pallas_call(
        functools.partial(_fox_kernel, tq=tq, tk=tk),
        out_shape=jax.ShapeDtypeStruct((nb, t, w), F32),
        grid=(nb, t // tq),
        in_specs=[pl.BlockSpec((1, tq, w), lambda b, i: (b, i, 0)),
                  pl.BlockSpec((1, w, t), lambda b, i: (b, 0, 0)),
                  pl.BlockSpec((1, w, t), lambda b, i: (b, 0, 0)),
                  pl.BlockSpec((1, tq, SMALL_W), lambda b, i: (b, i, 0)),
                  pl.BlockSpec((1, GROUP_HEADS, t), lambda b, i: (b, 0, 0)),
                  pl.BlockSpec((1, w), lambda b, i: (0, 0))],
        out_specs=pl.BlockSpec((1, tq, w), lambda b, i: (b, i, 0)),
        scratch_shapes=[pltpu.VMEM((w, t), BF16), pltpu.VMEM((w, t), BF16)],
        compiler_params=_cparams(("arbitrary", "arbitrary")),
        name="fox_prompt",
    )(q, kt, vt, small, frow, hg)


def _sb_kernel(q_ref, kt_ref, vt_ref, hg_ref, o_ref, ks_ref, vs_ref, *, tq, tk):
    qi = pl.program_id(1)

    @pl.when(qi == 0)
    def _():
        ks_ref[...] = kt_ref[0].astype(BF16)
        vs_ref[...] = vt_ref[0].astype(BF16)

    q_all = q_ref[0]
    qhs = [(q_all[:, h * HEAD_DIM:(h + 1) * HEAD_DIM] * QK_SCALE).astype(BF16) for h in range(GROUP_HEADS)]
    later = _tri01(tk, strict=True, lower=True)
    per_q = tq // tk

    def step(j, carry, masked):
        start = pl.multiple_of(j * tk, tk)
        if masked:
            before = (_iota((tq, tk), 1) + j * tk) < (_iota((tq, tk), 0) + qi * tq)
        heads = range(GROUP_HEADS)
        rows = [slice(h * HEAD_DIM, (h + 1) * HEAD_DIM) for h in heads]
        zs = [_dot(qhs[h], ks_ref[rows[h], pl.ds(start, tk)]) for h in heads]
        log_keeps = []
        for h in heads:
            z = zs[h]
            log_keep = -(jnp.maximum(z, 0.0) + _softplus_parts(z))
            if masked:
                log_keep = jnp.where(before, log_keep, 0.0)
            log_keeps.append(log_keep)
        afters = [_dot_exact_r(_dot, log_keeps[h], later, pieces=2) for h in heads]
        wgts = []
        for h in heads:
            wgt = jnp.exp(log_keeps[h] + zs[h] + (afters[h] + carry[h][0]))
            if masked:
                wgt = jnp.where(before, wgt, 0.0)
            wgts.append(wgt.astype(BF16))
        pv = [_dot_nt(wgts[h], vs_ref[rows[h], pl.ds(start, tk)]) for h in heads]
        return tuple((carry[h][0] + jnp.sum(log_keeps[h], axis=1, keepdims=True), carry[h][1] + pv[h]) for h in heads)

    carry = tuple((jnp.zeros((tq, 1), F32), jnp.zeros((tq, HEAD_DIM), F32)) for _ in range(GROUP_HEADS))
    for d in range(per_q):
        carry = step(qi * per_q + (per_q - 1 - d), carry, True)
    n_before = qi * per_q
    carry = lax.fori_loop(0, n_before, lambda i, c: step(n_before - 1 - i, c, False), carry)
    outs = [_rms(carry[h][1], hg_ref[:, h * HEAD_DIM:(h + 1) * HEAD_DIM]) for h in range(GROUP_HEADS)]
    o_ref[0] = jnp.concatenate(outs, axis=1)


def _sb_prompt(q, kt, vt, hg):
    nb, t, w = q.shape
    tq = min(ATTN_Q_TILE, t)
    tk = min(SB_KEY_TILE, t)
    return pl.pallas_call(
        functools.partial(_sb_kernel, tq=tq, tk=tk),
        out_shape=jax.ShapeDtypeStruct((nb, t, w), F32),
        grid=(nb, t // tq),
        in_specs=[pl.BlockSpec((1, tq, w), lambda b, i: (b, i, 0)),
                  pl.BlockSpec((1, w, t), lambda b, i: (b, 0, 0)),
                  pl.BlockSpec((1, w, t), lambda b, i: (b, 0, 0)),
                  pl.BlockSpec((1, w), lambda b, i: (0, 0))],
        out_specs=pl.BlockSpec((1, tq, w), lambda b, i: (b, i, 0)),
        scratch_shapes=[pltpu.VMEM((w, t), BF16), pltpu.VMEM((w, t), BF16)],
        compiler_params=_cparams(("arbitrary", "arbitrary")),
        name="sb_prompt",
    )(q, kt, vt, hg)


def _unit_lower_inverse(m_low, c):
    r, cc = _iota((c, c), 0), _iota((c, c), 1)
    eye = jnp.where(r == cc, 1.0, 0.0)
    ps = [eye - m for m in m_low]
    mps = list(m_low)
    for _ in range(int(math.log2(c)) - 1):
        mpb = [mp.astype(BF16) for mp in mps]
        mps = [_dot(b, b) for b in mpb]
        ps = [p + _dot(p.astype(BF16), mp.astype(BF16)) for p, mp in zip(ps, mps)]
    resid = [(eye - p) - _dot3(_dot, m, p) for m, p in zip(m_low, ps)]
    return [p + _dot(p.astype(BF16), rs.astype(BF16)) for p, rs in zip(ps, resid)]


def _gdn_kernel(x_ref, sm_ref, gate_ref, cb_ref, s0_ref, cw_ref, hg_ref,
                o_ref, sn_ref, cbn_ref, tail_ref, st_ref, *, c, n_valid):
    ci = pl.program_id(1)
    n_chunks = pl.num_programs(1)
    n_seq = x_ref.shape[0]

    @pl.when(ci == 0)
    def _():
        tail_ref[...] = cb_ref[...]
        st_ref[...] = s0_ref[...]

    rowi = _iota((c, GDN_CONV_DIM), 0)
    lane = _iota((c, SMALL_W), 1)
    cum_rows = _tri01(c, strict=False, lower=True)
    sel = jnp.where(_iota((16, SMALL_W), 1) == _iota((16, SMALL_W), 0) + SM_G, 1.0, 0.0).astype(BF16)
    eye_d = jnp.where(_iota((HEAD_DIM, HEAD_DIM), 0) == _iota((HEAD_DIM, HEAD_DIM), 1), 1.0, 0.0).astype(BF16)
    ri, cj = _iota((c, c), 0), _iota((c, c), 1)
    incl = ri >= cj
    strict = ri > cj
    n_last = c if n_valid is None else n_valid
    if n_valid is not None:
        valid = _iota((c, 1), 0) < n_valid

    seqs = range(n_seq)
    ys, sms = [], []
    for b in seqs:
        x = x_ref[b]
        tail = tail_ref[b]
        y = x * cw_ref[CONV_WIDTH - 1:CONV_WIDTH, :]
        tail_tiled = jnp.concatenate([tail] * (c // 8), axis=0)
        for s in range(1, CONV_WIDTH):
            shifted = jnp.where(rowi < s, pltpu.roll(tail_tiled, s, axis=0), pltpu.roll(x, s, axis=0))
            y = y + shifted * cw_ref[CONV_WIDTH - 1 - s:CONV_WIDTH - s, :]
        ys.append(_silu(y))

        if n_last % 8 == 0:
            new_tail = x[n_last - 8:n_last, :]
        else:
            cat = jnp.concatenate([tail, x[0:8, :]], axis=0)
            new_tail = pltpu.roll(cat, 16 - n_last, axis=0)[0:8, :]
        tail_ref[b] = x[c - 8:c, :]
        cbn_ref[b] = new_tail

        sm = sm_ref[b]
        if n_valid is not None:
            sm = jnp.where(valid, sm, 0.0)
        sms.append(sm)

    g_ls = [jnp.where((lane >= SM_G) & (lane < SM_G + GROUP_HEADS), sm, 0.0) for sm in sms]
    gcums = [_dot_exact_l(_dot, cum_rows, g_l) for g_l in g_ls]
    grows = [_dot_exact_l(_dot_nt, sel, gcum) for gcum in gcums]

    ones_head = _group_ones(GROUP_WIDTH, HEAD_DIM)
    q_alls = [y[:, 0:GROUP_WIDTH] for y in ys]
    k_alls = [y[:, GROUP_WIDTH:2 * GROUP_WIDTH] for y in ys]
    q_ss = [_dot_exact_r(_dot, q * q, ones_head) for q in q_alls]
    k_ss = [_dot_exact_r(_dot, k * k, ones_head) for k in k_alls]
    q_alls = [q * lax.rsqrt(ss + EPS) * QK_SCALE for q, ss in zip(q_alls, q_ss)]
    k_alls = [k * lax.rsqrt(ss + EPS) for k, ss in zip(k_alls, k_ss)]

    pairs = [(b, h) for b in seqs for h in range(GROUP_HEADS)]
    qs, ks, kbs, gcs, egcs, decays, vbs = [], [], [], [], [], [], []
    for b, h in pairs:
        lo = h * HEAD_DIM
        y, sm = ys[b], sms[b]
        qh = q_alls[b][:, lo:lo + HEAD_DIM]
        kh = k_alls[b][:, lo:lo + HEAD_DIM]
        vh = y[:, 2 * GROUP_WIDTH + lo:2 * GROUP_WIDTH + lo + HEAD_DIM]
        if n_valid is not None:
            qh = jnp.where(valid, qh, 0.0)
            kh = jnp.where(valid, kh, 0.0)
            vh = jnp.where(valid, vh, 0.0)
        gc = gcums[b][:, SM_G + h:SM_G + h + 1]
        gr = grows[b][h:h + 1, :]
        bh = sm[:, SM_BETA + h:SM_BETA + h + 1]
        kb = kh * bh
        egc = jnp.exp(gc)
        qs.append(qh)
        ks.append(kh)
        kbs.append(kb)
        gcs.append(gc)
        egcs.append(egc)
        decays.append(jnp.exp(jnp.where(incl, gc - gr, NEG_BIG)))
        vbs.append(vh * bh)

    khbs = [kh.astype(BF16) for kh in ks]
    m_lows = [jnp.where(strict, _dot_nt(kb.astype(BF16), khb) * dec, 0.0) for kb, khb, dec in zip(kbs, khbs, decays)]
    a_qks = [(_dot_nt(qh.astype(BF16), khb) * dec).astype(BF16) for qh, khb, dec in zip(qs, khbs, decays)]
    g_lasts = [gc[c - 1:c, :] for gc in gcs]
    kdec_ts = [_dot_nt(eye_d, (kh * jnp.exp(gl - gc)).astype(BF16)).astype(BF16)
               for kh, gl, gc in zip(ks, g_lasts, gcs)]
    invs = _unit_lower_inverse(m_lows, c)
    us = [_dot3(_dot, inv, vb) for inv, vb in zip(invs, vbs)]
    ws = [_dot3(_dot, inv, kb * egc) for inv, kb, egc in zip(invs, kbs, egcs)]
    states = [st_ref[b, h] for b, h in pairs]
    sbs = [s.astype(BF16) for s in states]
    o_states = [_dot((qh * egc).astype(BF16), sb) for qh, egc, sb in zip(qs, egcs, sbs)]
    v_news = [u - _dot(w.astype(BF16), sb) for u, w, sb in zip(us, ws, sbs)]
    vnbs = [v.astype(BF16) for v in v_news]
    o_locals = [_dot(a, vnb) for a, vnb in zip(a_qks, vnbs)]
    s_adds = [_dot(kt, vnb) for kt, vnb in zip(kdec_ts, vnbs)]
    for i, (b, h) in enumerate(pairs):
        st_ref[b, h] = states[i] * jnp.exp(g_lasts[i]) + s_adds[i]
    for b in seqs:
        gate = gate_ref[b]
        outs = []
        for h in range(GROUP_HEADS):
            i, lo = b * GROUP_HEADS + h, h * HEAD_DIM
            o = _rms(o_states[i] + o_locals[i], hg_ref[:, lo:lo + HEAD_DIM]) * _silu(gate[:, lo:lo + HEAD_DIM])
            outs.append(o)
        o_ref[b] = jnp.concatenate(outs, axis=1)

    @pl.when(ci == n_chunks - 1)
    def _():
        sn_ref[...] = st_ref[...]


def _gated_deltanet(gqkv, small, ggate, conv_buf8, s0, conv_w, hg, *, c, n_valid):
    nb, t, _ = gqkv.shape
    g = math.gcd(GDN_BATCH, nb)
    hd, gh = HEAD_DIM, GROUP_HEADS
    return pl.pallas_call(
        functools.partial(_gdn_kernel, c=c, n_valid=n_valid),
        out_shape=[jax.ShapeDtypeStruct((nb, t, GROUP_WIDTH), F32),
                   jax.ShapeDtypeStruct((nb, gh, hd, hd), F32),
                   jax.ShapeDtypeStruct((nb, 8, GDN_CONV_DIM), F32)],
        grid=(nb // g, t // c),
        in_specs=[pl.BlockSpec((g, c, GDN_CONV_DIM), lambda b, i: (b, i, 0)),
                  pl.BlockSpec((g, c, SMALL_W), lambda b, i: (b, i, 0)),
                  pl.BlockSpec((g, c, GROUP_WIDTH), lambda b, i: (b, i, 0)),
                  pl.BlockSpec((g, 8, GDN_CONV_DIM), lambda b, i: (b, 0, 0)),
                  pl.BlockSpec((g, gh, hd, hd), lambda b, i: (b, 0, 0, 0)),
                  pl.BlockSpec((CONV_WIDTH, GDN_CONV_DIM), lambda b, i: (0, 0)),
                  pl.BlockSpec((1, GROUP_WIDTH), lambda b, i: (0, 0))],
        out_specs=[pl.BlockSpec((g, c, GROUP_WIDTH), lambda b, i: (b, i, 0)),
                   pl.BlockSpec((g, gh, hd, hd), lambda b, i: (b, 0, 0, 0)),
                   pl.BlockSpec((g, 8, GDN_CONV_DIM), lambda b, i: (b, 0, 0))],
        scratch_shapes=[pltpu.VMEM((g, 8, GDN_CONV_DIM), F32),
                        pltpu.VMEM((g, gh, hd, hd), F32)],
        compiler_params=_cparams(("arbitrary", "arbitrary")),
        name="gated_deltanet",
    )(gqkv, small, ggate, conv_buf8, s0, conv_w, hg)


def _gmlp_kernel(uv_ref, vg_ref, w_ref, bias_ref, hg_ref, o_ref, v_ref):
    x = uv_ref[0]
    gelu = x * (0.5 * (1.0 + jnp.tanh(math.sqrt(2.0 / math.pi) * (x + 0.044715 * (x * x * x)))))
    u = gelu[:, :GROUP_WIDTH]
    v = _rms(gelu[:, GROUP_WIDTH:], vg_ref[...])
    v_ref[0] = v
    vb = v.astype(BF16)
    n = w_ref.shape[-1]
    chunks = x.shape[0] // n
    v_wide = jnp.concatenate([vb[i * n:(i + 1) * n, :] for i in range(chunks)], axis=1)
    tri = _iota((n, n), 0) >= _iota((n, n), 1)
    lane_group = (_iota((n, chunks * GROUP_WIDTH), 1) % GROUP_WIDTH) // HEAD_DIM
    mixed_wide = None
    for g in range(GROUP_HEADS):
        wg = jnp.where(tri, w_ref[g], 0.0).astype(BF16)
        part = jnp.where(lane_group == g, _dot(wg, v_wide), 0.0)
        mixed_wide = part if mixed_wide is None else mixed_wide + part
    mixed = jnp.concatenate([mixed_wide[:, i * GROUP_WIDTH:(i + 1) * GROUP_WIDTH] + bias_ref[...]
                             for i in range(chunks)], axis=0)
    o_ref[0] = _head_norm_lanes(u * mixed, hg_ref[...])


def _gmlp(guv, vg, w_s, bias_full, hg):
    nb, t, _ = guv.shape
    n = w_s.shape[-1]
    rows = n * math.gcd(GMLP_CHUNKS, t // n)
    return pl.pallas_call(
        _gmlp_kernel,
        out_shape=[jax.ShapeDtypeStruct((nb, t, GROUP_WIDTH), F32),
                   jax.ShapeDtypeStruct((nb, t, GROUP_WIDTH), F32)],
        grid=(nb, t // rows),
        in_specs=[pl.BlockSpec((1, rows, 2 * GROUP_WIDTH), lambda b, i: (b, i, 0)),
                  pl.BlockSpec((1, GROUP_WIDTH), lambda b, i: (0, 0)),
                  pl.BlockSpec((GROUP_HEADS, n, n), lambda b, i: (0, 0, 0)),
                  pl.BlockSpec((n, GROUP_WIDTH), lambda b, i: (0, 0)),
                  pl.BlockSpec((1, GROUP_WIDTH), lambda b, i: (0, 0))],
        out_specs=[pl.BlockSpec((1, rows, GROUP_WIDTH), lambda b, i: (b, i, 0)),
                   pl.BlockSpec((1, rows, GROUP_WIDTH), lambda b, i: (b, i, 0))],
        compiler_params=_cparams(("arbitrary", "arbitrary")),
        name="gmlp_gating",
    )(guv, vg, w_s, bias_full, hg)


def _post_kernel(of_ref, os_ref, og_ref, om_ref, x_ref, g1_ref, sh2_ref, sc2_ref, g2_ref, ng_ref,
                 wo_ref, w1_ref, w2_ref, y_ref, x1_ref, h_ref, acc_ref):
    f = pl.program_id(2)

    @pl.when(f == 0)
    def _():
        m = _dot(of_ref[0].astype(BF16), wo_ref[0 * GROUP_WIDTH:1 * GROUP_WIDTH, :])
        m = m + _dot(os_ref[0].astype(BF16), wo_ref[1 * GROUP_WIDTH:2 * GROUP_WIDTH, :])
        m = m + _dot(og_ref[0].astype(BF16), wo_ref[2 * GROUP_WIDTH:3 * GROUP_WIDTH, :])
        m = m + _dot(om_ref[0].astype(BF16), wo_ref[3 * GROUP_WIDTH:4 * GROUP_WIDTH, :])
        x1 = x_ref[0] + g1_ref[0] * _rms(m, ng_ref[1:2, :])
        x1_ref[...] = x1
        h = _rms(x1, ng_ref[2:3, :]) * (1.0 + sc2_ref[0]) + sh2_ref[0]
        h_ref[...] = h.astype(BF16)
        acc_ref[...] = jnp.zeros_like(acc_ref)

    a = jnp.maximum(_dot(h_ref[...], w1_ref[...]), 0.0)
    acc_ref[...] += _dot((a * a).astype(BF16), w2_ref[...])

    @pl.when(f == pl.num_programs(2) - 1)
    def _():
        y_ref[0] = x1_ref[...] + g2_ref[0] * _rms(acc_ref[...], ng_ref[3:4, :])


def _post(o_fox, o_sb, o_gdn, o_gmlp, x, gate1, shift2, scale2, gate2, norm_g, w_out, w1, w2, *, layer):
    nb, t, d = x.shape
    dff = w1.shape[-1]
    tm = min(MLP_ROW_TILE, t)
    tf = min(512, dff)
    tmod = gate1.shape[1]
    bm = tm if tmod == t else 1
    mod_spec = pl.BlockSpec((1, bm, d), (lambda b, i, f: (b, i, 0)) if tmod == t else (lambda b, i, f: (b, 0, 0)))
    o_spec = pl.BlockSpec((1, tm, GROUP_WIDTH), lambda b, i, f: (b, i, 0))
    return pl.pallas_call(
        _post_kernel,
        out_shape=jax.ShapeDtypeStruct((nb, t, d), F32),
        grid=(nb, t // tm, dff // tf),
        in_specs=[o_spec, o_spec, o_spec, o_spec,
                  pl.BlockSpec((1, tm, d), lambda b, i, f: (b, i, 0)),
                  mod_spec, mod_spec, mod_spec, mod_spec,
                  pl.BlockSpec((4, d), lambda b, i, f: (0, 0)),
                  pl.BlockSpec((None, d, d), lambda b, i, f: (layer, 0, 0)),
                  pl.BlockSpec((None, d, tf), lambda b, i, f: (layer, 0, f)),
                  pl.BlockSpec((None, tf, d), lambda b, i, f: (layer, f, 0))],
        out_specs=pl.BlockSpec((1, tm, d), lambda b, i, f: (b, i, 0)),
        scratch_shapes=[pltpu.VMEM((tm, d), F32), pltpu.VMEM((tm, d), BF16), pltpu.VMEM((tm, d), F32)],
        compiler_params=_cparams(("arbitrary", "arbitrary", "arbitrary")),
        name="outproj_mlp",
    )(o_fox, o_sb, o_gdn, o_gmlp, x, gate1, shift2, scale2, gate2, norm_g, w_out, w1, w2)


def _bias_rows(n_pages):
    return GROUP_HEADS * (n_pages + 2)


def _dec_bias_kernel(pt_ref, cache_ref, new_ref, later_ref, o_ref, buf_ref, sem, *, layer, n_pages):
    nb = o_ref.shape[0]
    rows = _bias_rows(n_pages)
    total = nb * n_pages

    def page_copy(i):
        b = i // n_pages
        p = i - b * n_pages
        dst = pl.multiple_of(b * rows + GROUP_HEADS * p, GROUP_HEADS)
        return pltpu.make_async_copy(cache_ref.at[layer, pt_ref[b, p]],
                                     buf_ref.at[pl.ds(dst, GROUP_HEADS), :], sem)

    def start(i, c):
        page_copy(i).start()
        return c

    def wait(i, c):
        page_copy(i).wait()
        return c

    unroll = math.gcd(8, total)
    lax.fori_loop(0, total, start, 0, unroll=unroll)
    lax.fori_loop(0, total, wait, 0, unroll=unroll)

    after = _tri01(PAGE_SIZE, strict=True, lower=True)
    later = later_ref[...]
    for b in range(nb):
        buf_ref[b * rows + GROUP_HEADS * n_pages:(b + 1) * rows, :] = new_ref[b]
        x = buf_ref[b * rows:(b + 1) * rows, :]
        within = _dot_exact_r(_dot, x, after)
        tot = jnp.broadcast_to(jnp.sum(x, axis=1, keepdims=True), x.shape)
        o_ref[b] = within + _dot_exact_l(_dot, later, tot)


def _dec_bias(page_table, cache_logf_t, new_rows, later, *, layer):
    nb, n_pages = page_table.shape
    rows = _bias_rows(n_pages)
    return pl.pallas_call(
        functools.partial(_dec_bias_kernel, layer=layer, n_pages=n_pages),
        out_shape=jax.ShapeDtypeStruct((nb, rows, PAGE_SIZE), F32),
        grid_spec=pltpu.PrefetchScalarGridSpec(
            num_scalar_prefetch=1,
            grid=(1,),
            in_specs=[pl.BlockSpec(memory_space=pl.ANY),
                      pl.BlockSpec((nb, 2 * GROUP_HEADS, PAGE_SIZE), lambda i, pt: (0, 0, 0)),
                      pl.BlockSpec((rows, rows), lambda i, pt: (0, 0))],
            out_specs=pl.BlockSpec((nb, rows, PAGE_SIZE), lambda i, pt: (0, 0, 0)),
            scratch_shapes=[pltpu.VMEM((nb * rows, PAGE_SIZE), F32), pltpu.SemaphoreType.DMA(())]),
        compiler_params=_cparams(("arbitrary",)),
        name="decode_forget_bias",
    )(page_table, cache_logf_t, new_rows, later)


def _rows_by_head(per_head, rows):
    per = rows // GROUP_HEADS
    rg = _iota((rows, PAGE_SIZE), 0) // per
    out = jnp.broadcast_to(per_head[0:1, :], (rows, PAGE_SIZE))
    for h in range(1, GROUP_HEADS):
        out = jnp.where(rg == h, jnp.broadcast_to(per_head[h:h + 1, :], (rows, PAGE_SIZE)), out)
    return out


def _dec_attn_kernel(pt_ref, cfk_ref, cfv_ref, csk_ref, csv_ref, qf_ref, qs_ref, nfk_ref, nfv_ref, nsk_ref, nsv_ref,
                     bias_ref, hgf_ref, hgs_ref, of_ref, os_ref,
                     buf_ref, sem, qbf_ref, qbs_ref, m_ref, l_ref, accf_ref, run_ref, accs_ref, ft_ref,
                     *, layer, pp, n_pages, n_q):
    b, g = pl.program_id(0), pl.program_id(1)
    n_b, n_g = pl.num_programs(0), pl.num_programs(1)
    step = b * n_g + g
    slot = lax.rem(step, 2)
    cache_refs = (cfk_ref, cfv_ref, csk_ref, csv_ref)

    def page_copies(seq, group, into):
        copies = []
        for r in range(pp):
            phys = pt_ref[seq, n_pages - 1 - (group * pp + r)]
            for i, cache in enumerate(cache_refs):
                copies.append(pltpu.make_async_copy(cache.at[layer, phys], buf_ref.at[into, i, r], sem.at[into]))
        return copies

    @pl.when(step == 0)
    def _():
        for cp in page_copies(0, 0, 0):
            cp.start()

    @pl.when(step + 1 < n_b * n_g)
    def _():
        wrap = g == n_g - 1
        for cp in page_copies(jnp.where(wrap, b + 1, b), jnp.where(wrap, 0, g + 1), 1 - slot):
            cp.start()

    for cp in page_copies(b, g, slot):
        cp.wait()

    rows = GROUP_HEADS * n_q
    row_head = _iota((rows, GROUP_WIDTH), 0) // n_q
    lane_head = _iota((rows, GROUP_WIDTH), 1) // HEAD_DIM
    own = row_head == lane_head
    later = _tri01(PAGE_SIZE, strict=True, lower=True)

    def attend(kfs, vfs, biases, fox_mask, kss, vss, sb_mask, transposed):
        qk, pv = (_dot, _dot_nt) if transposed else (_dot_nt, _dot)
        n = len(kfs)
        qf, qs = qbf_ref[...], qbs_ref[...]
        s_parts = [qk(qf, kf) for kf in kfs]
        zs = [qk(qs, ks) for ks in kss]

        s = jnp.concatenate(s_parts, axis=1) + (jnp.concatenate(biases, axis=1) - ft_ref[...])
        if fox_mask is not None:
            s = jnp.where(fox_mask, s, NEG_BIG)
        m = m_ref[...]
        m_new = jnp.maximum(m, jnp.max(s, axis=1, keepdims=True))
        alpha = jnp.exp(m - m_new)
        p = jnp.exp(s - m_new)
        l_ref[...] = alpha * l_ref[...] + jnp.sum(p, axis=1, keepdims=True)
        m_ref[...] = m_new

        log_keeps = []
        for z in zs:
            log_keep = -(jnp.maximum(z, 0.0) + _softplus_parts(z))
            if sb_mask is not None:
                log_keep = jnp.where(sb_mask, log_keep, 0.0)
            log_keeps.append(log_keep)
        afters = [_dot_exact_r(_dot, lk, later, pieces=2) for lk in log_keeps]
        fox_parts = [pv(p[:, r * PAGE_SIZE:(r + 1) * PAGE_SIZE].astype(BF16), vfs[r]) for r in range(n)]

        run = run_ref[...]
        wgts = []
        for r in range(n):
            wgt = jnp.exp(log_keeps[r] + zs[r] + (afters[r] + run))
            if sb_mask is not None:
                wgt = jnp.where(sb_mask, wgt, 0.0)
            wgts.append(wgt.astype(BF16))
            run = run + jnp.sum(log_keeps[r], axis=1, keepdims=True)
        run_ref[...] = run
        sb_parts = [pv(wgts[r], vss[r]) for r in range(n)]

        accf_ref[...] = alpha * accf_ref[...] + functools.reduce(lambda a, b: a + b, fox_parts)
        accs_ref[...] = accs_ref[...] + functools.reduce(lambda a, b: a + b, sb_parts)

    @pl.when(g == 0)
    def _():
        qbf_ref[...] = jnp.where(own, qf_ref[0] * QK_SCALE, 0.0).astype(BF16)
        qbs_ref[...] = jnp.where(own, qs_ref[0] * QK_SCALE, 0.0).astype(BF16)
        m_ref[...] = jnp.full_like(m_ref, NEG_BIG)
        l_ref[...] = jnp.zeros_like(l_ref)
        accf_ref[...] = jnp.zeros_like(accf_ref)
        accs_ref[...] = jnp.zeros_like(accs_ref)
        run_ref[...] = jnp.zeros_like(run_ref)
        bias16 = _rows_by_head(bias_ref[0, GROUP_HEADS * n_pages:GROUP_HEADS * (n_pages + 1), :], rows)
        qpos = _iota((rows, PAGE_SIZE), 0) % n_q
        kpos = _iota((rows, PAGE_SIZE), 1)
        ft_ref[...] = jnp.sum(jnp.where(kpos == qpos, bias16, 0.0), axis=1, keepdims=True)
        pad = jnp.zeros((PAGE_SIZE - 8, GROUP_WIDTH), F32)

        def page_of(r):
            return jnp.concatenate([r[0], pad], axis=0).astype(BF16)

        attend([page_of(nfk_ref)], [page_of(nfv_ref)], [bias16], (kpos <= qpos) & (kpos < n_q),
               [page_of(nsk_ref)], [page_of(nsv_ref)], kpos < qpos, False)

    first = pl.multiple_of(GROUP_HEADS * (n_pages - (g + 1) * pp), GROUP_HEADS * pp)
    slab = bias_ref[0, pl.ds(first, GROUP_HEADS * pp), :]
    biases = [_rows_by_head(slab[GROUP_HEADS * (pp - 1 - r):GROUP_HEADS * (pp - r), :], rows) for r in range(pp)]
    page = lambda r, i: buf_ref[slot, i, r].astype(BF16)
    attend([page(r, 0) for r in range(pp)], [page(r, 1) for r in range(pp)], biases, None,
           [page(r, 2) for r in range(pp)], [page(r, 3) for r in range(pp)], None, True)

    @pl.when(g == pl.num_programs(1) - 1)
    def _():
        def fold(acc):
            a = jnp.where(own, acc, 0.0)
            out = a[0:n_q, :]
            for h in range(1, GROUP_HEADS):
                out = out + a[h * n_q:(h + 1) * n_q, :]
            return out

        of_ref[0] = _head_norm_lanes(fold(accf_ref[...] / l_ref[...]), hgf_ref[...])
        os_ref[0] = _head_norm_lanes(fold(accs_ref[...]), hgs_ref[...])


def _dec_attention(page_table, cfk, cfv, csk, csv, qf16, qs16, nfk, nfv, nsk, nsv, bias, hgf, hgs, *, layer, n_q):
    nb, n_pages = page_table.shape
    pp = math.gcd(DEC_PAGES, n_pages)
    assert pp % 2 == 0
    n_groups = n_pages // pp
    rows = GROUP_HEADS * n_q

    cache_specs = [pl.BlockSpec(memory_space=pl.ANY)] * 4
    cache_args = [cfk, cfv, csk, csv]
    per_b3 = lambda b, g, pt: (b, 0, 0)
    const2 = lambda b, g, pt: (0, 0)
    return pl.pallas_call(
        functools.partial(_dec_attn_kernel, layer=layer, pp=pp, n_pages=n_pages, n_q=n_q),
        out_shape=[jax.ShapeDtypeStruct((nb, n_q, GROUP_WIDTH), F32)] * 2,
        grid_spec=pltpu.PrefetchScalarGridSpec(
            num_scalar_prefetch=1,
            grid=(nb, n_groups),
            in_specs=cache_specs + [
                pl.BlockSpec((1, rows, GROUP_WIDTH), per_b3),
                pl.BlockSpec((1, rows, GROUP_WIDTH), per_b3),
                pl.BlockSpec((1, 8, GROUP_WIDTH), per_b3),
                pl.BlockSpec((1, 8, GROUP_WIDTH), per_b3),
                pl.BlockSpec((1, 8, GROUP_WIDTH), per_b3),
                pl.BlockSpec((1, 8, GROUP_WIDTH), per_b3),
                pl.BlockSpec((1, _bias_rows(n_pages), PAGE_SIZE), per_b3),
                pl.BlockSpec((1, GROUP_WIDTH), const2),
                pl.BlockSpec((1, GROUP_WIDTH), const2)],
            out_specs=[pl.BlockSpec((1, n_q, GROUP_WIDTH), per_b3)] * 2,
            scratch_shapes=[pltpu.VMEM((2, 4, pp, GROUP_WIDTH, PAGE_SIZE), F32), pltpu.SemaphoreType.DMA((2,)),
                            pltpu.VMEM((rows, GROUP_WIDTH), BF16), pltpu.VMEM((rows, GROUP_WIDTH), BF16),
                            pltpu.VMEM((rows, 1), F32), pltpu.VMEM((rows, 1), F32),
                            pltpu.VMEM((rows, GROUP_WIDTH), F32),
                            pltpu.VMEM((rows, 1), F32), pltpu.VMEM((rows, GROUP_WIDTH), F32),
                            pltpu.VMEM((rows, 1), F32)]),
        compiler_params=_cparams(("arbitrary", "arbitrary")),
        name="decode_attention",
    )(page_table, *cache_args, qf16, qs16, nfk, nfv, nsk, nsv, bias, hgf, hgs)


def _permute_w_in(w_in):
    gw, gh = GROUP_WIDTH, GROUP_HEADS
    o = 0
    fq = w_in[..., o:o + gw]; o += gw
    fk = w_in[..., o:o + gw]; o += gw
    fv = w_in[..., o:o + gw]; o += gw
    ff = w_in[..., o:o + gh]; o += gh
    sq = w_in[..., o:o + gw]; o += gw
    sk = w_in[..., o:o + gw]; o += gw
    sv = w_in[..., o:o + gw]; o += gw
    gqkv = w_in[..., o:o + GDN_CONV_DIM]; o += GDN_CONV_DIM
    ga = w_in[..., o:o + gh]; o += gh
    gb = w_in[..., o:o + gh]; o += gh
    ggate = w_in[..., o:o + gw]; o += gw
    guv = w_in[..., o:o + 2 * gw]
    pad = jnp.zeros(w_in.shape[:-1] + (SMALL_W - 4 * gh,), w_in.dtype)
    w = jnp.concatenate([fq, fk, fv, sq, sk, sv, gqkv, ggate, guv, ff, ga, gb, ff, pad], axis=-1).astype(BF16)
    wt = jnp.swapaxes(jnp.concatenate([fk, fv, sk, sv], axis=-1), -1, -2).astype(BF16)
    return w, wt


def _small_params(b_forget, a_log, dt_bias):
    z4 = jnp.zeros((GROUP_HEADS,), F32)
    zpad = jnp.zeros((SMALL_W - 4 * GROUP_HEADS,), F32)
    bias = jnp.concatenate([b_forget, dt_bias, z4, b_forget, zpad])
    alog = jnp.concatenate([z4, a_log, z4, z4, zpad])
    return jnp.zeros((8, SMALL_W), F32).at[0].set(bias).at[1].set(alog)


def _later_pages_matrix(n_pages):
    idx = jnp.arange(_bias_rows(n_pages))
    page, head = idx // GROUP_HEADS, idx % GROUP_HEADS
    return ((head[:, None] == head[None, :]) & (page[None, :] > page[:, None])).astype(BF16)


def _pages_transposed(cache):
    depth, n_phys = cache.shape[:2]
    return jnp.transpose(cache, (0, 1, 3, 4, 2)).reshape(depth, n_phys, GROUP_WIDTH, PAGE_SIZE)


def _heads_last(a_t, t):
    return jnp.transpose(a_t.reshape(a_t.shape[0], GROUP_HEADS, HEAD_DIM, t), (0, 3, 1, 2))


def kernel(x_prompt, x_sample, cache_fox_k, cache_fox_v, cache_fox_logf, cache_sb_k, cache_sb_v, state_gdn, state_gdn_conv, page_table, c_prompt, c_sample, ada_w, ada_b, norm_g, w_in, b_forget, gdn_conv_w, gdn_a_log, gdn_dt_bias, gmlp_v_norm_g, gmlp_w_s, gmlp_b_s, head_norm_g, w_out, w_ff1, w_ff2):
    depth = ada_w.shape[0]
    n_p, t_p, d = x_prompt.shape
    n_s, t_s, _ = x_sample.shape
    n_phys = cache_fox_k.shape[1]
    rows_s = n_s * t_s
    gw, gh = GROUP_WIDTH, GROUP_HEADS

    mods = _modulation(jnp.concatenate([c_prompt, c_sample], axis=0), ada_w, ada_b)
    mods_p = mods[:, :n_p].reshape(depth, n_p, 1, N_MOD, d)
    mods_s = jnp.repeat(mods[:, n_p:], t_s, axis=1).reshape(depth, 1, rows_s, N_MOD, d)

    n_pages = page_table.shape[1]
    cfk, cfv = _pages_transposed(cache_fox_k), _pages_transposed(cache_fox_v)
    csk, csv = _pages_transposed(cache_sb_k), _pages_transposed(cache_sb_v)
    clogf = jnp.swapaxes(cache_fox_logf, 2, 3)
    later = _later_pages_matrix(n_pages)

    zeros_cb = jnp.zeros((n_p, 8, GDN_CONV_DIM), F32)
    zeros_s0 = jnp.zeros((n_p, gh, HEAD_DIM, HEAD_DIM), F32)
    eye_s = jnp.eye(n_s, dtype=F32)

    w_all, wt_all = _permute_w_in(w_in)
    wo_all, w1_all, w2_all = w_out.astype(BF16), w_ff1.astype(BF16), w_ff2.astype(BF16)

    x_p = x_prompt
    x_s = x_sample.reshape(1, rows_s, d)
    outs_p, outs_s = [], []
    for l in range(depth):
        sp = _small_params(b_forget[l], gdn_a_log[l], gdn_dt_bias[l])
        hg = head_norm_g[l].reshape(4, 1, gw)
        ng = norm_g[l]
        bias_p = jnp.repeat(gmlp_b_s[l].T, HEAD_DIM, axis=1)
        vg = gmlp_v_norm_g[l].reshape(1, gw)

        mp = [mods_p[l, :, :, i] for i in range(N_MOD)]
        fq, fk, fv, sq, sk, sv, gqkv, ggate, guv, small = _in_projection(
            x_p, mp[0], mp[1], ng[0:1], w_all, wt_all, sp, layer=l, cumsum=True, kv_transposed=True)
        frow = jnp.swapaxes(small[:, :, SM_CUM:SM_CUM + gh], 1, 2)
        o_fox = _fox_prompt(fq, fk, fv, small, frow, hg[0])
        o_sb = _sb_prompt(sq, sk, sv, hg[1])
        o_gdn, s_new, cb_new = _gated_deltanet(gqkv, small, ggate, zeros_cb, zeros_s0, gdn_conv_w[l], hg[2],
                                               c=GDN_CHUNK, n_valid=None)
        o_gmlp, _ = _gmlp(guv, vg, gmlp_w_s[l], bias_p, hg[3])
        x_p = _post(o_fox, o_sb, o_gdn, o_gmlp, x_p, mp[2], mp[3], mp[4], mp[5], ng, wo_all, w1_all, w2_all, layer=l)
        outs_p.append((_heads_last(fk, t_p), _heads_last(fv, t_p), small[:, :, SM_LOGF:SM_LOGF + gh],
                       _heads_last(sk, t_p), _heads_last(sv, t_p), s_new, cb_new[:, 8 - (CONV_WIDTH - 1):]))

        ms = [mods_s[l, :, :, i] for i in range(N_MOD)]
        fq, fk, fv, sq, sk, sv, gqkv, ggate, guv, small = _in_projection(
            x_s, ms[0], ms[1], ng[0:1], w_all, wt_all, sp, layer=l, cumsum=False, kv_transposed=False)
        logf_new = small[0, :, SM_LOGF:SM_LOGF + gh]
        new_rows = jnp.pad(jnp.swapaxes(logf_new.reshape(n_s, t_s, gh), 1, 2),
                           ((0, 0), (0, gh), (0, PAGE_SIZE - t_s)))
        bias = _dec_bias(page_table, clogf, new_rows, later, layer=l)
        per_seq = lambda a: a.reshape(n_s, t_s, gw)
        pad8 = lambda a: jnp.pad(per_seq(a), ((0, 0), (0, 8 - t_s), (0, 0)))
        tile_q = lambda a: jnp.tile(per_seq(a), (1, gh, 1))
        o_fox, o_sb = _dec_attention(page_table, cfk, cfv, csk, csv, tile_q(fq), tile_q(sq),
                                     pad8(fk), pad8(fv), pad8(sk), pad8(sv), bias, hg[0], hg[1],
                                     layer=l, n_q=t_s)
        cs = GDN_CHUNK_SMALL
        padc = lambda a: jnp.pad(a.reshape(n_s, t_s, a.shape[-1]), ((0, 0), (0, cs - t_s), (0, 0)))
        cb8 = jnp.pad(state_gdn_conv[l], ((0, 0), (8 - (CONV_WIDTH - 1), 0), (0, 0)))
        o_gdn, s_new, cb_new = _gated_deltanet(padc(gqkv), padc(small), padc(ggate), cb8, state_gdn[l],
                                               gdn_conv_w[l], hg[2], c=cs, n_valid=t_s)
        o_gdn = o_gdn[:, :t_s].reshape(1, rows_s, gw)
        w_blk = jnp.einsum('ab,gij->gaibj', eye_s, gmlp_w_s[l][:, :t_s, :t_s]).reshape(gh, rows_s, rows_s)
        bias_s = jnp.tile(bias_p[:t_s], (n_s, 1))
        o_gmlp, v_rows = _gmlp(guv, vg, w_blk, bias_s, hg[3])
        x_s = _post(o_fox.reshape(1, rows_s, gw), o_sb.reshape(1, rows_s, gw), o_gdn, o_gmlp, x_s,
                    ms[2], ms[3], ms[4], ms[5], ng, wo_all, w1_all, w2_all, layer=l)
        outs_s.append((per_seq(fk[0]), per_seq(fv[0]), logf_new.reshape(n_s, t_s, gh), per_seq(sk[0]),
                       per_seq(sv[0]), s_new, cb_new[:, 8 - (CONV_WIDTH - 1):], per_seq(v_rows[0])))

    def stacked(states, i, shape=None):
        a = jnp.stack([s[i] for s in states])
        return a if shape is None else a.reshape(shape)

    hp = (depth, n_p, t_p, gh, HEAD_DIM)
    hs = (depth, n_s, t_s, gh, HEAD_DIM)
    return (x_p, x_s.reshape(n_s, t_s, d),
            stacked(outs_p, 0, hp), stacked(outs_p, 1, hp), stacked(outs_p, 2), stacked(outs_p, 3, hp),
            stacked(outs_p, 4, hp), stacked(outs_p, 5), stacked(outs_p, 6),
            stacked(outs_s, 0, hs), stacked(outs_s, 1, hs), stacked(outs_s, 2), stacked(outs_s, 3, hs),
            stacked(outs_s, 4, hs), stacked(outs_s, 5), stacked(outs_s, 6), stacked(outs_s, 7))
```

```python
import functools
import math

import jax
import jax.numpy as jnp
from jax import lax
from jax.experimental import pallas as pl
from jax.experimental.pallas import tpu as pltpu

F32 = jnp.float32
BF16 = jnp.bfloat16

HEAD_DIM = 64
GROUP_HEADS = 4
GROUP_WIDTH = GROUP_HEADS * HEAD_DIM
CONV_WIDTH = 4
GDN_CONV_DIM = 3 * GROUP_WIDTH
GDN_CHUNK = 64
GDN_CHUNK_SMALL = 16
MLP_CHUNK = 128
PAGE_SIZE = 128
N_MOD = 6
EPS = 1e-6
QK_SCALE = HEAD_DIM ** -0.5
NEG_BIG = -1e30
LOG2_E = 1.4426950408889634

SM_LOGF = 0
SM_G = 4
SM_BETA = 8
SM_CUM = 12
SMALL_W = 128

C_FQ, C_FK, C_FV, C_SQ, C_SK, C_SV = (i * GROUP_WIDTH for i in range(6))
C_GQKV = 6 * GROUP_WIDTH
C_GGATE = C_GQKV + GDN_CONV_DIM
C_GUV = C_GGATE + GROUP_WIDTH
C_SMALL = C_GUV + 2 * GROUP_WIDTH
IN_PAD_WIDTH = C_SMALL + SMALL_W

VMEM_LIMIT = 56 * 1024 * 1024

ATTN_Q_TILE = 512
FOX_KEY_TILE = 512
SB_KEY_TILE = 256
GDN_BATCH = 8
DEC_PAGES = 16
GMLP_CHUNKS = 4
MLP_FF_TILE = 1024
INPROJ_ROW_TILE = 512
MLP_ROW_TILE = 1024


def _cparams(sem):
    return pltpu.CompilerParams(dimension_semantics=sem, vmem_limit_bytes=VMEM_LIMIT)


def _dot(a, b):
    return jnp.dot(a, b, preferred_element_type=F32)


def _dot_nt(a, b):
    return lax.dot_general(a, b, (((1,), (1,)), ((), ())), preferred_element_type=F32)


def _split2(a):
    hi = a.astype(BF16)
    lo = (a - hi.astype(F32)).astype(BF16)
    return hi, lo


def _split3(a):
    hi = a.astype(BF16)
    r = a - hi.astype(F32)
    mid = r.astype(BF16)
    lo = (r - mid.astype(F32)).astype(BF16)
    return hi, mid, lo


def _dot3(dot, a, b):
    ah, al = _split2(a)
    bh, bl = _split2(b)
    return dot(ah, bh) + (dot(ah, bl) + dot(al, bh))


def _dot_exact_l(dot, a01, b, pieces=3):
    parts = _split3(b) if pieces == 3 else _split2(b)
    out = dot(a01, parts[0])
    for p in parts[1:]:
        out = out + dot(a01, p)
    return out


def _split2_trunc(a):
    hi = pltpu.bitcast(pltpu.bitcast(a, jnp.uint32) & jnp.uint32(0xFFFF0000), F32)
    return hi.astype(BF16), (a - hi).astype(BF16)


def _dot_exact_r(dot, a, b01, pieces=3):
    parts = _split3(a) if pieces == 3 else _split2_trunc(a)
    out = dot(parts[0], b01)
    for p in parts[1:]:
        out = out + dot(p, b01)
    return out


def _softplus_parts(x):
    return jnp.log(1.0 + jnp.exp2(jnp.abs(x) * (-LOG2_E)))


def _sigmoid(x):
    return 1.0 / (1.0 + jnp.exp(-x))


def _silu(x):
    return x * _sigmoid(x)


def _rms(x, g):
    return x * lax.rsqrt(jnp.mean(x * x, axis=-1, keepdims=True) + EPS) * g


def _iota(shape, dim):
    return lax.broadcasted_iota(jnp.int32, shape, dim)


def _tri01(n, strict, lower, dtype=BF16):
    r, c = _iota((n, n), 0), _iota((n, n), 1)
    if lower:
        m = (r > c) if strict else (r >= c)
    else:
        m = (r < c) if strict else (r <= c)
    return jnp.where(m, 1.0, 0.0).astype(dtype)


def _group_ones(n, group):
    r, c = _iota((n, n), 0), _iota((n, n), 1)
    return jnp.where((r // group) == (c // group), 1.0, 0.0).astype(BF16)


def _head_norm_lanes(x, gain):
    ones = _group_ones(x.shape[-1], HEAD_DIM)
    ms = _dot_exact_r(_dot, x * x, ones) * (1.0 / HEAD_DIM)
    return x * lax.rsqrt(ms + EPS) * gain


def _mod_kernel(c_ref, w_ref, b_ref, o_ref):
    a = _silu(c_ref[...]).astype(BF16)
    o_ref[0] = _dot(a, w_ref[0].astype(BF16)) + b_ref[0]


def _modulation(c_all, ada_w, ada_b):
    depth, d, n = ada_w.shape
    rows = c_all.shape[0]
    tn = 1536 if n % 1536 == 0 else n
    return pl.pallas_call(
        _mod_kernel,
        out_shape=jax.ShapeDtypeStruct((depth, rows, n), F32),
        grid=(depth, n // tn),
        in_specs=[pl.BlockSpec((rows, d), lambda l, j: (0, 0)),
                  pl.BlockSpec((1, d, tn), lambda l, j: (l, 0, j)),
                  pl.BlockSpec((1, 1, tn), lambda l, j: (l, 0, j))],
        out_specs=pl.BlockSpec((1, rows, tn), lambda l, j: (l, 0, j)),
        compiler_params=_cparams(("arbitrary", "arbitrary")),
        name="adaln_mod",
    )(c_all, ada_w, ada_b.reshape(depth, 1, n))


def _inproj_kernel(x_ref, sh_ref, sc_ref, g_ref, w_ref, wt_ref, sp_ref,
                   fq, fk, fv, sq, sk, sv, gqkv, ggate, guv, small, carry_ref, *, cumsum, kv_transposed):
    x = x_ref[0]
    h = _rms(x, g_ref[...]) * (1.0 + sc_ref[0]) + sh_ref[0]
    hb = h.astype(BF16)

    def seg(off, n):
        return _dot(hb, w_ref[:, off:off + n])

    def seg_t(i):
        return _dot_nt(wt_ref[i * GROUP_WIDTH:(i + 1) * GROUP_WIDTH, :], hb)

    fq[0] = seg(C_FQ, GROUP_WIDTH)
    sq[0] = seg(C_SQ, GROUP_WIDTH)
    if kv_transposed:
        fk[0] = seg_t(0)
        fv[0] = seg_t(1)
        sk[0] = seg_t(2)
        sv[0] = seg_t(3)
    else:
        fk[0] = seg(C_FK, GROUP_WIDTH)
        fv[0] = seg(C_FV, GROUP_WIDTH)
        sk[0] = seg(C_SK, GROUP_WIDTH)
        sv[0] = seg(C_SV, GROUP_WIDTH)
    gqkv[0] = seg(C_GQKV, GDN_CONV_DIM)
    ggate[0] = seg(C_GGATE, GROUP_WIDTH)
    guv[0] = seg(C_GUV, 2 * GROUP_WIDTH)

    z = seg(C_SMALL, SMALL_W)
    tm = z.shape[0]
    lane = _iota(z.shape, 1)
    zz = z + sp_ref[0:1, :]
    t = _softplus_parts(zz)
    logf = jnp.minimum(zz, 0.0) - t
    gdec = -jnp.exp(sp_ref[1:2, :]) * (jnp.maximum(zz, 0.0) + t)
    beta = _sigmoid(z)
    out = jnp.where(lane < SM_G, logf, jnp.where(lane < SM_BETA, gdec, jnp.where(lane < SM_CUM, beta, 0.0)))
    if cumsum:
        @pl.when(pl.program_id(1) == 0)
        def _():
            carry_ref[...] = jnp.zeros_like(carry_ref)

        lf = jnp.where((lane >= SM_CUM) & (lane < SM_CUM + GROUP_HEADS), logf, 0.0)
        cum = _dot_exact_l(_dot, _tri01(tm, strict=False, lower=True), lf) + carry_ref[0:1, :]
        carry_ref[0:1, :] = cum[tm - 1:tm, :]
        out = out + cum
    small[0] = out


def _in_projection(x, shift, scale, g, w, wt, sp, *, layer, cumsum, kv_transposed):
    nb, t, d = x.shape
    tm = min(INPROJ_ROW_TILE, t)
    tmod = shift.shape[1]
    bm = tm if tmod == t else 1
    mod_spec = pl.BlockSpec((1, bm, d), (lambda b, i: (b, i, 0)) if tmod == t else (lambda b, i: (b, 0, 0)))
    gw = GROUP_WIDTH
    row = lambda wd: (jax.ShapeDtypeStruct((nb, t, wd), F32), pl.BlockSpec((1, tm, wd), lambda b, i: (b, i, 0)))
    col = (jax.ShapeDtypeStruct((nb, gw, t), F32), pl.BlockSpec((1, gw, tm), lambda b, i: (b, 0, i)))
    kv = col if kv_transposed else row(gw)
    outs = [row(gw), kv, kv, row(gw), kv, kv, row(GDN_CONV_DIM), row(gw), row(2 * gw), row(SMALL_W)]
    return pl.pallas_call(
        functools.partial(_inproj_kernel, cumsum=cumsum, kv_transposed=kv_transposed),
        out_shape=[o[0] for o in outs],
        grid=(nb, t // tm),
        in_specs=[pl.BlockSpec((1, tm, d), lambda b, i: (b, i, 0)),
                  mod_spec, mod_spec,
                  pl.BlockSpec((1, d), lambda b, i: (0, 0)),
                  pl.BlockSpec((None, d, IN_PAD_WIDTH), lambda b, i: (layer, 0, 0)),
                  pl.BlockSpec((None, 4 * gw, d), lambda b, i: (layer, 0, 0)),
                  pl.BlockSpec((8, SMALL_W), lambda b, i: (0, 0))],
        out_specs=[o[1] for o in outs],
        scratch_shapes=[pltpu.VMEM((8, SMALL_W), F32)],
        compiler_params=_cparams(("arbitrary", "arbitrary")),
        name="in_projection",
    )(x, shift, scale, g, w, wt, sp)


def _fox_kernel(q_ref, kt_ref, vt_ref, smq_ref, frow_ref, hg_ref, o_ref, ks_ref, vs_ref, *, tq, tk):
    qi = pl.program_id(1)

    @pl.when(qi == 0)
    def _():
        ks_ref[...] = kt_ref[0].astype(BF16)
        vs_ref[...] = vt_ref[0].astype(BF16)

    q_all = q_ref[0]
    sm = smq_ref[0]
    qhs = [(q_all[:, h * HEAD_DIM:(h + 1) * HEAD_DIM] * QK_SCALE).astype(BF16) for h in range(GROUP_HEADS)]
    f_qs = [sm[:, SM_CUM + h:SM_CUM + h + 1] for h in range(GROUP_HEADS)]

    def step(j, carry, masked):
        start = pl.multiple_of(j * tk, tk)
        heads = range(GROUP_HEADS)
        rows = [slice(h * HEAD_DIM, (h + 1) * HEAD_DIM) for h in heads]
        qk = [_dot(qhs[h], ks_ref[rows[h], pl.ds(start, tk)]) for h in heads]
        if masked:
            visible = (_iota((tq, tk), 1) + j * tk) <= (_iota((tq, tk), 0) + qi * tq)
        stats, ps = [], []
        for h in heads:
            m, l, _ = carry[h]
            f_k = frow_ref[0, h:h + 1, pl.ds(start, tk)]
            s = qk[h] + (f_qs[h] - f_k)
            if masked:
                s = jnp.where(visible, s, NEG_BIG)
            m_new = jnp.maximum(m, jnp.max(s, axis=1, keepdims=True))
            alpha = jnp.exp(m - m_new)
            p = jnp.exp(s - m_new)
            stats.append((m_new, alpha, alpha * l + jnp.sum(p, axis=1, keepdims=True)))
            ps.append(p.astype(BF16))
        pv = [_dot_nt(ps[h], vs_ref[rows[h], pl.ds(start, tk)]) for h in heads]
        return tuple((stats[h][0], stats[h][2], stats[h][1] * carry[h][2] + pv[h]) for h in heads)

    per_q = tq // tk
    init = tuple((jnp.full((tq, 1), NEG_BIG, F32), jnp.zeros((tq, 1), F32), jnp.zeros((tq, HEAD_DIM), F32))
                 for _ in range(GROUP_HEADS))
    carry = lax.fori_loop(0, qi * per_q, functools.partial(step, masked=False), init)
    for d in range(per_q):
        carry = step(qi * per_q + d, carry, True)
    outs = []
    for h in range(GROUP_HEADS):
        m, l, acc = carry[h]
        outs.append(_rms(acc / l, hg_ref[:, h * HEAD_DIM:(h + 1) * HEAD_DIM]))
    o_ref[0] = jnp.concatenate(outs, axis=1)


def _fox_prompt(q, kt, vt, small, frow, hg):
    nb, t, w = q.shape
    tq = min(ATTN_Q_TILE, t)
    tk = min(FOX_KEY_TILE, t)
    return pl.pallas_call(
        functools.partial(_fox_kernel, tq=tq, tk=tk),
        out_shape=jax.ShapeDtypeStruct((nb, t, w), F32),
        grid=(nb, t // tq),
        in_specs=[pl.BlockSpec((1, tq, w), lambda b, i: (b, i, 0)),
                  pl.BlockSpec((1, w, t), lambda b, i: (b, 0, 0)),
                  pl.BlockSpec((1, w, t), lambda b, i: (b, 0, 0)),
                  pl.BlockSpec((1, tq, SMALL_W), lambda b, i: (b, i, 0)),
                  pl.BlockSpec((1, GROUP_HEADS, t), lambda b, i: (b, 0, 0)),
                  pl.BlockSpec((1, w), lambda b, i: (0, 0))],
        out_specs=pl.BlockSpec((1, tq, w), lambda b, i: (b, i, 0)),
        scratch_shapes=[pltpu.VMEM((w, t), BF16), pltpu.VMEM((w, t), BF16)],
        compiler_params=_cparams(("arbitrary", "arbitrary")),
        name="fox_prompt",
    )(q, kt, vt, small, frow, hg)


def _sb_kernel(q_ref, kt_ref, vt_ref, hg_ref, o_ref, ks_ref, vs_ref, *, tq, tk):
    qi = pl.program_id(1)

    @pl.when(qi == 0)
    def _():
        ks_ref[...] = kt_ref[0].astype(BF16)
        vs_ref[...] = vt_ref[0].astype(BF16)

    q_all = q_ref[0]
    qhs = [(q_all[:, h * HEAD_DIM:(h + 1) * HEAD_DIM] * QK_SCALE).astype(BF16) for h in range(GROUP_HEADS)]
    later = _tri01(tk, strict=True, lower=True)
    per_q = tq // tk

    def step(j, carry, masked):
        start = pl.multiple_of(j * tk, tk)
        if masked:
            before = (_iota((tq, tk), 1) + j * tk) < (_iota((tq, tk), 0) + qi * tq)
        heads = range(GROUP_HEADS)
        rows = [slice(h * HEAD_DIM, (h + 1) * HEAD_DIM) for h in heads]
        zs = [_dot(qhs[h], ks_ref[rows[h], pl.ds(start, tk)]) for h in heads]
        log_keeps = []
        for h in heads:
            z = zs[h]
            log_keep = -(jnp.maximum(z, 0.0) + _softplus_parts(z))
            if masked:
                log_keep = jnp.where(before, log_keep, 0.0)
            log_keeps.append(log_keep)
        afters = [_dot_exact_r(_dot, log_keeps[h], later, pieces=2) for h in heads]
        wgts = []
        for h in heads:
            wgt = jnp.exp(log_keeps[h] + zs[h] + (afters[h] + carry[h][0]))
            if masked:
                wgt = jnp.where(before, wgt, 0.0)
            wgts.append(wgt.astype(BF16))
        pv = [_dot_nt(wgts[h], vs_ref[rows[h], pl.ds(start, tk)]) for h in heads]
        return tuple((carry[h][0] + jnp.sum(log_keeps[h], axis=1, keepdims=True), carry[h][1] + pv[h]) for h in heads)

    carry = tuple((jnp.zeros((tq, 1), F32), jnp.zeros((tq, HEAD_DIM), F32)) for _ in range(GROUP_HEADS))
    for d in range(per_q):
        carry = step(qi * per_q + (per_q - 1 - d), carry, True)
    n_before = qi * per_q
    carry = lax.fori_loop(0, n_before, lambda i, c: step(n_before - 1 - i, c, False), carry)
    outs = [_rms(carry[h][1], hg_ref[:, h * HEAD_DIM:(h + 1) * HEAD_DIM]) for h in range(GROUP_HEADS)]
    o_ref[0] = jnp.concatenate(outs, axis=1)


def _sb_prompt(q, kt, vt, hg):
    nb, t, w = q.shape
    tq = min(ATTN_Q_TILE, t)
    tk = min(SB_KEY_TILE, t)
    return pl.pallas_call(
        functools.partial(_sb_kernel, tq=tq, tk=tk),
        out_shape=jax.ShapeDtypeStruct((nb, t, w), F32),
        grid=(nb, t // tq),
        in_specs=[pl.BlockSpec((1, tq, w), lambda b, i: (b, i, 0)),
                  pl.BlockSpec((1, w, t), lambda b, i: (b, 0, 0)),
                  pl.BlockSpec((1, w, t), lambda b, i: (b, 0, 0)),
                  pl.BlockSpec((1, w), lambda b, i: (0, 0))],
        out_specs=pl.BlockSpec((1, tq, w), lambda b, i: (b, i, 0)),
        scratch_shapes=[pltpu.VMEM((w, t), BF16), pltpu.VMEM((w, t), BF16)],
        compiler_params=_cparams(("arbitrary", "arbitrary")),
        name="sb_prompt",
    )(q, kt, vt, hg)


def _unit_lower_inverse(m_low, c):
    r, cc = _iota((c, c), 0), _iota((c, c), 1)
    eye = jnp.where(r == cc, 1.0, 0.0)
    ps = [eye - m for m in m_low]
    mps = list(m_low)
    for _ in range(int(math.log2(c)) - 1):
        mps = [_dot3(_dot, mp, mp) for mp in mps]
        ps = [p + _dot3(_dot, p, mp) for p, mp in zip(ps, mps)]
    for _ in range(2):
        resid = [(eye - p) - _dot3(_dot, m, p) for m, p in zip(m_low, ps)]
        ps = [p + _dot3(_dot, p, rs) for p, rs in zip(ps, resid)]
    return ps


def _gdn_kernel(x_ref, sm_ref, gate_ref, cb_ref, s0_ref, cw_ref, hg_ref,
                o_ref, sn_ref, cbn_ref, tail_ref, st_ref, *, c, n_valid):
    ci = pl.program_id(1)
    n_chunks = pl.num_programs(1)
    n_seq = x_ref.shape[0]

    @pl.when(ci == 0)
    def _():
        tail_ref[...] = cb_ref[...]
        st_ref[...] = s0_ref[...]

    rowi = _iota((c, GDN_CONV_DIM), 0)
    lane = _iota((c, SMALL_W), 1)
    cum_rows = _tri01(c, strict=False, lower=True)
    sel = jnp.where(_iota((16, SMALL_W), 1) == _iota((16, SMALL_W), 0) + SM_G, 1.0, 0.0).astype(BF16)
    eye_d = jnp.where(_iota((HEAD_DIM, HEAD_DIM), 0) == _iota((HEAD_DIM, HEAD_DIM), 1), 1.0, 0.0).astype(BF16)
    ri, cj = _iota((c, c), 0), _iota((c, c), 1)
    incl = ri >= cj
    strict = ri > cj
    n_last = c if n_valid is None else n_valid
    if n_valid is not None:
        valid = _iota((c, 1), 0) < n_valid

    seqs = range(n_seq)
    ys, sms = [], []
    for b in seqs:
        x = x_ref[b]
        tail = tail_ref[b]
        y = x * cw_ref[CONV_WIDTH - 1:CONV_WIDTH, :]
        tail_tiled = jnp.concatenate([tail] * (c // 8), axis=0)
        for s in range(1, CONV_WIDTH):
            shifted = jnp.where(rowi < s, pltpu.roll(tail_tiled, s, axis=0), pltpu.roll(x, s, axis=0))
            y = y + shifted * cw_ref[CONV_WIDTH - 1 - s:CONV_WIDTH - s, :]
        ys.append(_silu(y))

        if n_last % 8 == 0:
            new_tail = x[n_last - 8:n_last, :]
        else:
            cat = jnp.concatenate([tail, x[0:8, :]], axis=0)
            new_tail = pltpu.roll(cat, 16 - n_last, axis=0)[0:8, :]
        tail_ref[b] = x[c - 8:c, :]
        cbn_ref[b] = new_tail

        sm = sm_ref[b]
        if n_valid is not None:
            sm = jnp.where(valid, sm, 0.0)
        sms.append(sm)

    g_ls = [jnp.where((lane >= SM_G) & (lane < SM_G + GROUP_HEADS), sm, 0.0) for sm in sms]
    gcums = [_dot_exact_l(_dot, cum_rows, g_l) for g_l in g_ls]
    grows = [_dot_exact_l(_dot_nt, sel, gcum) for gcum in gcums]

    ones_head = _group_ones(GROUP_WIDTH, HEAD_DIM)
    q_alls = [y[:, 0:GROUP_WIDTH] for y in ys]
    k_alls = [y[:, GROUP_WIDTH:2 * GROUP_WIDTH] for y in ys]
    q_ss = [_dot_exact_r(_dot, q * q, ones_head) for q in q_alls]
    k_ss = [_dot_exact_r(_dot, k * k, ones_head) for k in k_alls]
    q_alls = [q * lax.rsqrt(ss + EPS) * QK_SCALE for q, ss in zip(q_alls, q_ss)]
    k_alls = [k * lax.rsqrt(ss + EPS) for k, ss in zip(k_alls, k_ss)]

    pairs = [(b, h) for b in seqs for h in range(GROUP_HEADS)]
    qs, ks, kbs, gcs, egcs, decays, vbs = [], [], [], [], [], [], []
    for b, h in pairs:
        lo = h * HEAD_DIM
        y, sm = ys[b], sms[b]
        qh = q_alls[b][:, lo:lo + HEAD_DIM]
        kh = k_alls[b][:, lo:lo + HEAD_DIM]
        vh = y[:, 2 * GROUP_WIDTH + lo:2 * GROUP_WIDTH + lo + HEAD_DIM]
        if n_valid is not None:
            qh = jnp.where(valid, qh, 0.0)
            kh = jnp.where(valid, kh, 0.0)
            vh = jnp.where(valid, vh, 0.0)
        gc = gcums[b][:, SM_G + h:SM_G + h + 1]
        gr = grows[b][h:h + 1, :]
        bh = sm[:, SM_BETA + h:SM_BETA + h + 1]
        kb = kh * bh
        egc = jnp.exp(gc)
        qs.append(qh)
        ks.append(kh)
        kbs.append(kb)
        gcs.append(gc)
        egcs.append(egc)
        decays.append(jnp.exp(jnp.where(incl, gc - gr, NEG_BIG)))
        vbs.append(vh * bh)

    khbs = [kh.astype(BF16) for kh in ks]
    m_lows = [jnp.where(strict, _dot_nt(kb.astype(BF16), khb) * dec, 0.0) for kb, khb, dec in zip(kbs, khbs, decays)]
    a_qks = [(_dot_nt(qh.astype(BF16), khb) * dec).astype(BF16) for qh, khb, dec in zip(qs, khbs, decays)]
    g_lasts = [gc[c - 1:c, :] for gc in gcs]
    kdec_ts = [_dot_nt(eye_d, (kh * jnp.exp(gl - gc)).astype(BF16)).astype(BF16)
               for kh, gl, gc in zip(ks, g_lasts, gcs)]
    invs = _unit_lower_inverse(m_lows, c)
    us = [_dot3(_dot, inv, vb) for inv, vb in zip(invs, vbs)]
    ws = [_dot3(_dot, inv, kb * egc) for inv, kb, egc in zip(invs, kbs, egcs)]
    states = [st_ref[b, h] for b, h in pairs]
    sbs = [s.astype(BF16) for s in states]
    o_states = [_dot((qh * egc).astype(BF16), sb) for qh, egc, sb in zip(qs, egcs, sbs)]
    v_news = [u - _dot(w.astype(BF16), sb) for u, w, sb in zip(us, ws, sbs)]
    vnbs = [v.astype(BF16) for v in v_news]
    o_locals = [_dot(a, vnb) for a, vnb in zip(a_qks, vnbs)]
    s_adds = [_dot(kt, vnb) for kt, vnb in zip(kdec_ts, vnbs)]
    for i, (b, h) in enumerate(pairs):
        st_ref[b, h] = states[i] * jnp.exp(g_lasts[i]) + s_adds[i]
    for b in seqs:
        gate = gate_ref[b]
        outs = []
        for h in range(GROUP_HEADS):
            i, lo = b * GROUP_HEADS + h, h * HEAD_DIM
            o = _rms(o_states[i] + o_locals[i], hg_ref[:, lo:lo + HEAD_DIM]) * _silu(gate[:, lo:lo + HEAD_DIM])
            outs.append(o)
        o_ref[b] = jnp.concatenate(outs, axis=1)

    @pl.when(ci == n_chunks - 1)
    def _():
        sn_ref[...] = st_ref[...]


def _gated_deltanet(gqkv, small, ggate, conv_buf8, s0, conv_w, hg, *, c, n_valid):
    nb, t, _ = gqkv.shape
    g = math.gcd(GDN_BATCH, nb)
    hd, gh = HEAD_DIM, GROUP_HEADS
    return pl.pallas_call(
        functools.partial(_gdn_kernel, c=c, n_valid=n_valid),
        out_shape=[jax.ShapeDtypeStruct((nb, t, GROUP_WIDTH), F32),
                   jax.ShapeDtypeStruct((nb, gh, hd, hd), F32),
                   jax.ShapeDtypeStruct((nb, 8, GDN_CONV_DIM), F32)],
        grid=(nb // g, t // c),
        in_specs=[pl.BlockSpec((g, c, GDN_CONV_DIM), lambda b, i: (b, i, 0)),
                  pl.BlockSpec((g, c, SMALL_W), lambda b, i: (b, i, 0)),
                  pl.BlockSpec((g, c, GROUP_WIDTH), lambda b, i: (b, i, 0)),
                  pl.BlockSpec((g, 8, GDN_CONV_DIM), lambda b, i: (b, 0, 0)),
                  pl.BlockSpec((g, gh, hd, hd), lambda b, i: (b, 0, 0, 0)),
                  pl.BlockSpec((CONV_WIDTH, GDN_CONV_DIM), lambda b, i: (0, 0)),
                  pl.BlockSpec((1, GROUP_WIDTH), lambda b, i: (0, 0))],
        out_specs=[pl.BlockSpec((g, c, GROUP_WIDTH), lambda b, i: (b, i, 0)),
                   pl.BlockSpec((g, gh, hd, hd), lambda b, i: (b, 0, 0, 0)),
                   pl.BlockSpec((g, 8, GDN_CONV_DIM), lambda b, i: (b, 0, 0))],
        scratch_shapes=[pltpu.VMEM((g, 8, GDN_CONV_DIM), F32),
                        pltpu.VMEM((g, gh, hd, hd), F32)],
        compiler_params=_cparams(("arbitrary", "arbitrary")),
        name="gated_deltanet",
    )(gqkv, small, ggate, conv_buf8, s0, conv_w, hg)


def _gmlp_kernel(uv_ref, vg_ref, w_ref, bias_ref, hg_ref, o_ref, v_ref):
    x = uv_ref[0]
    gelu = x * (0.5 * (1.0 + jnp.tanh(math.sqrt(2.0 / math.pi) * (x + 0.044715 * (x * x * x)))))
    u = gelu[:, :GROUP_WIDTH]
    v = _rms(gelu[:, GROUP_WIDTH:], vg_ref[...])
    v_ref[0] = v
    vb = v.astype(BF16)
    n = w_ref.shape[-1]
    chunks = x.shape[0] // n
    v_wide = jnp.concatenate([vb[i * n:(i + 1) * n, :] for i in range(chunks)], axis=1)
    tri = _iota((n, n), 0) >= _iota((n, n), 1)
    lane_group = (_iota((n, chunks * GROUP_WIDTH), 1) % GROUP_WIDTH) // HEAD_DIM
    mixed_wide = None
    for g in range(GROUP_HEADS):
        wg = jnp.where(tri, w_ref[g], 0.0).astype(BF16)
        part = jnp.where(lane_group == g, _dot(wg, v_wide), 0.0)
        mixed_wide = part if mixed_wide is None else mixed_wide + part
    mixed = jnp.concatenate([mixed_wide[:, i * GROUP_WIDTH:(i + 1) * GROUP_WIDTH] + bias_ref[...]
                             for i in range(chunks)], axis=0)
    o_ref[0] = _head_norm_lanes(u * mixed, hg_ref[...])


def _gmlp(guv, vg, w_s, bias_full, hg):
    nb, t, _ = guv.shape
    n = w_s.shape[-1]
    rows = n * math.gcd(GMLP_CHUNKS, t // n)
    return pl.pallas_call(
        _gmlp_kernel,
        out_shape=[jax.ShapeDtypeStruct((nb, t, GROUP_WIDTH), F32),
                   jax.ShapeDtypeStruct((nb, t, GROUP_WIDTH), F32)],
        grid=(nb, t // rows),
        in_specs=[pl.BlockSpec((1, rows, 2 * GROUP_WIDTH), lambda b, i: (b, i, 0)),
                  pl.BlockSpec((1, GROUP_WIDTH), lambda b, i: (0, 0)),
                  pl.BlockSpec((GROUP_HEADS, n, n), lambda b, i: (0, 0, 0)),
                  pl.BlockSpec((n, GROUP_WIDTH), lambda b, i: (0, 0)),
                  pl.BlockSpec((1, GROUP_WIDTH), lambda b, i: (0, 0))],
        out_specs=[pl.BlockSpec((1, rows, GROUP_WIDTH), lambda b, i: (b, i, 0)),
                   pl.BlockSpec((1, rows, GROUP_WIDTH), lambda b, i: (b, i, 0))],
        compiler_params=_cparams(("arbitrary", "arbitrary")),
        name="gmlp_gating",
    )(guv, vg, w_s, bias_full, hg)


def _post_kernel(of_ref, os_ref, og_ref, om_ref, x_ref, g1_ref, sh2_ref, sc2_ref, g2_ref, ng_ref,
                 wo_ref, w1_ref, w2_ref, y_ref, x1_ref, h_ref, acc_ref):
    f = pl.program_id(2)

    @pl.when(f == 0)
    def _():
        m = _dot(of_ref[0].astype(BF16), wo_ref[0 * GROUP_WIDTH:1 * GROUP_WIDTH, :])
        m = m + _dot(os_ref[0].astype(BF16), wo_ref[1 * GROUP_WIDTH:2 * GROUP_WIDTH, :])
        m = m + _dot(og_ref[0].astype(BF16), wo_ref[2 * GROUP_WIDTH:3 * GROUP_WIDTH, :])
        m = m + _dot(om_ref[0].astype(BF16), wo_ref[3 * GROUP_WIDTH:4 * GROUP_WIDTH, :])
        x1 = x_ref[0] + g1_ref[0] * _rms(m, ng_ref[1:2, :])
        x1_ref[...] = x1
        h = _rms(x1, ng_ref[2:3, :]) * (1.0 + sc2_ref[0]) + sh2_ref[0]
        h_ref[...] = h.astype(BF16)
        acc_ref[...] = jnp.zeros_like(acc_ref)

    a = jnp.maximum(_dot(h_ref[...], w1_ref[...]), 0.0)
    acc_ref[...] += _dot((a * a).astype(BF16), w2_ref[...])

    @pl.when(f == pl.num_programs(2) - 1)
    def _():
        y_ref[0] = x1_ref[...] + g2_ref[0] * _rms(acc_ref[...], ng_ref[3:4, :])


def _post(o_fox, o_sb, o_gdn, o_gmlp, x, gate1, shift2, scale2, gate2, norm_g, w_out, w1, w2, *, layer):
    nb, t, d = x.shape
    dff = w1.shape[-1]
    tm = min(MLP_ROW_TILE, t)
    tf = min(MLP_FF_TILE, dff)
    tmod = gate1.shape[1]
    bm = tm if tmod == t else 1
    mod_spec = pl.BlockSpec((1, bm, d), (lambda b, i, f: (b, i, 0)) if tmod == t else (lambda b, i, f: (b, 0, 0)))
    o_spec = pl.BlockSpec((1, tm, GROUP_WIDTH), lambda b, i, f: (b, i, 0))
    return pl.pallas_call(
        _post_kernel,
        out_shape=jax.ShapeDtypeStruct((nb, t, d), F32),
        grid=(nb, t // tm, dff // tf),
        in_specs=[o_spec, o_spec, o_spec, o_spec,
                  pl.BlockSpec((1, tm, d), lambda b, i, f: (b, i, 0)),
                  mod_spec, mod_spec, mod_spec, mod_spec,
                  pl.BlockSpec((4, d), lambda b, i, f: (0, 0)),
                  pl.BlockSpec((None, d, d), lambda b, i, f: (layer, 0, 0)),
                  pl.BlockSpec((None, d, tf), lambda b, i, f: (layer, 0, f)),
                  pl.BlockSpec((None, tf, d), lambda b, i, f: (layer, f, 0))],
        out_specs=pl.BlockSpec((1, tm, d), lambda b, i, f: (b, i, 0)),
        scratch_shapes=[pltpu.VMEM((tm, d), F32), pltpu.VMEM((tm, d), BF16), pltpu.VMEM((tm, d), F32)],
        compiler_params=_cparams(("arbitrary", "arbitrary", "arbitrary")),
        name="outproj_mlp",
    )(o_fox, o_sb, o_gdn, o_gmlp, x, gate1, shift2, scale2, gate2, norm_g, w_out, w1, w2)


def _bias_rows(n_pages):
    return GROUP_HEADS * (n_pages + 2)


def _dec_bias_kernel(pt_ref, cache_ref, new_ref, later_ref, o_ref, buf_ref, sem, *, layer, n_pages):
    nb = o_ref.shape[0]
    rows = _bias_rows(n_pages)
    total = nb * n_pages

    def page_copy(i):
        b = i // n_pages
        p = i - b * n_pages
        dst = pl.multiple_of(b * rows + GROUP_HEADS * p, GROUP_HEADS)
        return pltpu.make_async_copy(cache_ref.at[layer, pt_ref[b, p]],
                                     buf_ref.at[pl.ds(dst, GROUP_HEADS), :], sem)

    def start(i, c):
        page_copy(i).start()
        return c

    def wait(i, c):
        page_copy(i).wait()
        return c

    unroll = math.gcd(8, total)
    lax.fori_loop(0, total, start, 0, unroll=unroll)
    lax.fori_loop(0, total, wait, 0, unroll=unroll)

    after = _tri01(PAGE_SIZE, strict=True, lower=True)
    later = later_ref[...]
    for b in range(nb):
        buf_ref[b * rows + GROUP_HEADS * n_pages:(b + 1) * rows, :] = new_ref[b]
        x = buf_ref[b * rows:(b + 1) * rows, :]
        within = _dot_exact_r(_dot, x, after)
        tot = jnp.broadcast_to(jnp.sum(x, axis=1, keepdims=True), x.shape)
        o_ref[b] = within + _dot_exact_l(_dot, later, tot)


def _dec_bias(page_table, cache_logf_t, new_rows, later, *, layer):
    nb, n_pages = page_table.shape
    rows = _bias_rows(n_pages)
    return pl.pallas_call(
        functools.partial(_dec_bias_kernel, layer=layer, n_pages=n_pages),
        out_shape=jax.ShapeDtypeStruct((nb, rows, PAGE_SIZE), F32),
        grid_spec=pltpu.PrefetchScalarGridSpec(
            num_scalar_prefetch=1,
            grid=(1,),
            in_specs=[pl.BlockSpec(memory_space=pl.ANY),
                      pl.BlockSpec((nb, 2 * GROUP_HEADS, PAGE_SIZE), lambda i, pt: (0, 0, 0)),
                      pl.BlockSpec((rows, rows), lambda i, pt: (0, 0))],
            out_specs=pl.BlockSpec((nb, rows, PAGE_SIZE), lambda i, pt: (0, 0, 0)),
            scratch_shapes=[pltpu.VMEM((nb * rows, PAGE_SIZE), F32), pltpu.SemaphoreType.DMA(())]),
        compiler_params=_cparams(("arbitrary",)),
        name="decode_forget_bias",
    )(page_table, cache_logf_t, new_rows, later)


def _rows_by_head(per_head, rows):
    per = rows // GROUP_HEADS
    rg = _iota((rows, PAGE_SIZE), 0) // per
    out = jnp.broadcast_to(per_head[0:1, :], (rows, PAGE_SIZE))
    for h in range(1, GROUP_HEADS):
        out = jnp.where(rg == h, jnp.broadcast_to(per_head[h:h + 1, :], (rows, PAGE_SIZE)), out)
    return out


def _dec_attn_kernel(pt_ref, cfk_ref, cfv_ref, csk_ref, csv_ref, qf_ref, qs_ref, nfk_ref, nfv_ref, nsk_ref, nsv_ref,
                     bias_ref, hgf_ref, hgs_ref, of_ref, os_ref,
                     buf_ref, sem, qbf_ref, qbs_ref, m_ref, l_ref, accf_ref, run_ref, accs_ref, ft_ref,
                     *, layer, pp, n_pages, n_q):
    b, g = pl.program_id(0), pl.program_id(1)
    n_b, n_g = pl.num_programs(0), pl.num_programs(1)
    step = b * n_g + g
    slot = lax.rem(step, 2)
    cache_refs = (cfk_ref, cfv_ref, csk_ref, csv_ref)

    def page_copies(seq, group, into):
        copies = []
        for r in range(pp):
            phys = pt_ref[seq, n_pages - 1 - (group * pp + r)]
            for i, cache in enumerate(cache_refs):
                copies.append(pltpu.make_async_copy(cache.at[layer, phys], buf_ref.at[into, i, r], sem.at[into]))
        return copies

    @pl.when(step == 0)
    def _():
        for cp in page_copies(0, 0, 0):
            cp.start()

    @pl.when(step + 1 < n_b * n_g)
    def _():
        wrap = g == n_g - 1
        for cp in page_copies(jnp.where(wrap, b + 1, b), jnp.where(wrap, 0, g + 1), 1 - slot):
            cp.start()

    for cp in page_copies(b, g, slot):
        cp.wait()

    rows = GROUP_HEADS * n_q
    row_head = _iota((rows, GROUP_WIDTH), 0) // n_q
    lane_head = _iota((rows, GROUP_WIDTH), 1) // HEAD_DIM
    own = row_head == lane_head
    later = _tri01(PAGE_SIZE, strict=True, lower=True)

    def attend(kfs, vfs, biases, fox_mask, kss, vss, sb_mask, transposed):
        qk, pv = (_dot, _dot_nt) if transposed else (_dot_nt, _dot)
        n = len(kfs)
        qf, qs = qbf_ref[...], qbs_ref[...]
        s_parts = [qk(qf, kf) for kf in kfs]
        zs = [qk(qs, ks) for ks in kss]

        s = jnp.concatenate(s_parts, axis=1) + (jnp.concatenate(biases, axis=1) - ft_ref[...])
        if fox_mask is not None:
            s = jnp.where(fox_mask, s, NEG_BIG)
        m = m_ref[...]
        m_new = jnp.maximum(m, jnp.max(s, axis=1, keepdims=True))
        alpha = jnp.exp(m - m_new)
        p = jnp.exp(s - m_new)
        l_ref[...] = alpha * l_ref[...] + jnp.sum(p, axis=1, keepdims=True)
        m_ref[...] = m_new

        log_keeps = []
        for z in zs:
            log_keep = -(jnp.maximum(z, 0.0) + _softplus_parts(z))
            if sb_mask is not None:
                log_keep = jnp.where(sb_mask, log_keep, 0.0)
            log_keeps.append(log_keep)
        afters = [_dot_exact_r(_dot, lk, later, pieces=2) for lk in log_keeps]
        fox_parts = [pv(p[:, r * PAGE_SIZE:(r + 1) * PAGE_SIZE].astype(BF16), vfs[r]) for r in range(n)]

        run = run_ref[...]
        wgts = []
        for r in range(n):
            wgt = jnp.exp(log_keeps[r] + zs[r] + (afters[r] + run))
            if sb_mask is not None:
                wgt = jnp.where(sb_mask, wgt, 0.0)
            wgts.append(wgt.astype(BF16))
            run = run + jnp.sum(log_keeps[r], axis=1, keepdims=True)
        run_ref[...] = run
        sb_parts = [pv(wgts[r], vss[r]) for r in range(n)]

        accf_ref[...] = alpha * accf_ref[...] + functools.reduce(lambda a, b: a + b, fox_parts)
        accs_ref[...] = accs_ref[...] + functools.reduce(lambda a, b: a + b, sb_parts)

    @pl.when(g == 0)
    def _():
        qbf_ref[...] = jnp.where(own, qf_ref[0] * QK_SCALE, 0.0).astype(BF16)
        qbs_ref[...] = jnp.where(own, qs_ref[0] * QK_SCALE, 0.0).astype(BF16)
        m_ref[...] = jnp.full_like(m_ref, NEG_BIG)
        l_ref[...] = jnp.zeros_like(l_ref)
        accf_ref[...] = jnp.zeros_like(accf_ref)
        accs_ref[...] = jnp.zeros_like(accs_ref)
        run_ref[...] = jnp.zeros_like(run_ref)
        bias16 = _rows_by_head(bias_ref[0, GROUP_HEADS * n_pages:GROUP_HEADS * (n_pages + 1), :], rows)
        qpos = _iota((rows, PAGE_SIZE), 0) % n_q
        kpos = _iota((rows, PAGE_SIZE), 1)
        ft_ref[...] = jnp.sum(jnp.where(kpos == qpos, bias16, 0.0), axis=1, keepdims=True)
        pad = jnp.zeros((PAGE_SIZE - 8, GROUP_WIDTH), F32)

        def page_of(r):
            return jnp.concatenate([r[0], pad], axis=0).astype(BF16)

        attend([page_of(nfk_ref)], [page_of(nfv_ref)], [bias16], (kpos <= qpos) & (kpos < n_q),
               [page_of(nsk_ref)], [page_of(nsv_ref)], kpos < qpos, False)

    first = pl.multiple_of(GROUP_HEADS * (n_pages - (g + 1) * pp), GROUP_HEADS * pp)
    slab = bias_ref[0, pl.ds(first, GROUP_HEADS * pp), :]
    biases = [_rows_by_head(slab[GROUP_HEADS * (pp - 1 - r):GROUP_HEADS * (pp - r), :], rows) for r in range(pp)]
    page = lambda r, i: buf_ref[slot, i, r].astype(BF16)
    attend([page(r, 0) for r in range(pp)], [page(r, 1) for r in range(pp)], biases, None,
           [page(r, 2) for r in range(pp)], [page(r, 3) for r in range(pp)], None, True)

    @pl.when(g == pl.num_programs(1) - 1)
    def _():
        def fold(acc):
            a = jnp.where(own, acc, 0.0)
            out = a[0:n_q, :]
            for h in range(1, GROUP_HEADS):
                out = out + a[h * n_q:(h + 1) * n_q, :]
            return out

        of_ref[0] = _head_norm_lanes(fold(accf_ref[...] / l_ref[...]), hgf_ref[...])
        os_ref[0] = _head_norm_lanes(fold(accs_ref[...]), hgs_ref[...])


def _dec_attention(page_table, cfk, cfv, csk, csv, qf16, qs16, nfk, nfv, nsk, nsv, bias, hgf, hgs, *, layer, n_q):
    nb, n_pages = page_table.shape
    pp = math.gcd(DEC_PAGES, n_pages)
    assert pp % 2 == 0
    n_groups = n_pages // pp
    rows = GROUP_HEADS * n_q

    cache_specs = [pl.BlockSpec(memory_space=pl.ANY)] * 4
    cache_args = [cfk, cfv, csk, csv]
    per_b3 = lambda b, g, pt: (b, 0, 0)
    const2 = lambda b, g, pt: (0, 0)
    return pl.pallas_call(
        functools.partial(_dec_attn_kernel, layer=layer, pp=pp, n_pages=n_pages, n_q=n_q),
        out_shape=[jax.ShapeDtypeStruct((nb, n_q, GROUP_WIDTH), F32)] * 2,
        grid_spec=pltpu.PrefetchScalarGridSpec(
            num_scalar_prefetch=1,
            grid=(nb, n_groups),
            in_specs=cache_specs + [
                pl.BlockSpec((1, rows, GROUP_WIDTH), per_b3),
                pl.BlockSpec((1, rows, GROUP_WIDTH), per_b3),
                pl.BlockSpec((1, 8, GROUP_WIDTH), per_b3),
                pl.BlockSpec((1, 8, GROUP_WIDTH), per_b3),
                pl.BlockSpec((1, 8, GROUP_WIDTH), per_b3),
                pl.BlockSpec((1, 8, GROUP_WIDTH), per_b3),
                pl.BlockSpec((1, _bias_rows(n_pages), PAGE_SIZE), per_b3),
                pl.BlockSpec((1, GROUP_WIDTH), const2),
                pl.BlockSpec((1, GROUP_WIDTH), const2)],
            out_specs=[pl.BlockSpec((1, n_q, GROUP_WIDTH), per_b3)] * 2,
            scratch_shapes=[pltpu.VMEM((2, 4, pp, GROUP_WIDTH, PAGE_SIZE), F32), pltpu.SemaphoreType.DMA((2,)),
                            pltpu.VMEM((rows, GROUP_WIDTH), BF16), pltpu.VMEM((rows, GROUP_WIDTH), BF16),
                            pltpu.VMEM((rows, 1), F32), pltpu.VMEM((rows, 1), F32),
                            pltpu.VMEM((rows, GROUP_WIDTH), F32),
                            pltpu.VMEM((rows, 1), F32), pltpu.VMEM((rows, GROUP_WIDTH), F32),
                            pltpu.VMEM((rows, 1), F32)]),
        compiler_params=_cparams(("arbitrary", "arbitrary")),
        name="decode_attention",
    )(page_table, *cache_args, qf16, qs16, nfk, nfv, nsk, nsv, bias, hgf, hgs)


def _permute_w_in(w_in):
    gw, gh = GROUP_WIDTH, GROUP_HEADS
    o = 0
    fq = w_in[..., o:o + gw]; o += gw
    fk = w_in[..., o:o + gw]; o += gw
    fv = w_in[..., o:o + gw]; o += gw
    ff = w_in[..., o:o + gh]; o += gh
    sq = w_in[..., o:o + gw]; o += gw
    sk = w_in[..., o:o + gw]; o += gw
    sv = w_in[..., o:o + gw]; o += gw
    gqkv = w_in[..., o:o + GDN_CONV_DIM]; o += GDN_CONV_DIM
    ga = w_in[..., o:o + gh]; o += gh
    gb = w_in[..., o:o + gh]; o += gh
    ggate = w_in[..., o:o + gw]; o += gw
    guv = w_in[..., o:o + 2 * gw]
    pad = jnp.zeros(w_in.shape[:-1] + (SMALL_W - 4 * gh,), w_in.dtype)
    w = jnp.concatenate([fq, fk, fv, sq, sk, sv, gqkv, ggate, guv, ff, ga, gb, ff, pad], axis=-1).astype(BF16)
    wt = jnp.swapaxes(jnp.concatenate([fk, fv, sk, sv], axis=-1), -1, -2).astype(BF16)
    return w, wt


def _small_params(b_forget, a_log, dt_bias):
    z4 = jnp.zeros((GROUP_HEADS,), F32)
    zpad = jnp.zeros((SMALL_W - 4 * GROUP_HEADS,), F32)
    bias = jnp.concatenate([b_forget, dt_bias, z4, b_forget, zpad])
    alog = jnp.concatenate([z4, a_log, z4, z4, zpad])
    return jnp.zeros((8, SMALL_W), F32).at[0].set(bias).at[1].set(alog)


def _later_pages_matrix(n_pages):
    idx = jnp.arange(_bias_rows(n_pages))
    page, head = idx // GROUP_HEADS, idx % GROUP_HEADS
    return ((head[:, None] == head[None, :]) & (page[None, :] > page[:, None])).astype(BF16)


def _pages_transposed(cache):
    depth, n_phys = cache.shape[:2]
    return jnp.transpose(cache, (0, 1, 3, 4, 2)).reshape(depth, n_phys, GROUP_WIDTH, PAGE_SIZE)


def _heads_last(a_t, t):
    return jnp.transpose(a_t.reshape(a_t.shape[0], GROUP_HEADS, HEAD_DIM, t), (0, 3, 1, 2))


def kernel(x_prompt, x_sample, cache_fox_k, cache_fox_v, cache_fox_logf, cache_sb_k, cache_sb_v, state_gdn, state_gdn_conv, page_table, c_prompt, c_sample, ada_w, ada_b, norm_g, w_in, b_forget, gdn_conv_w, gdn_a_log, gdn_dt_bias, gmlp_v_norm_g, gmlp_w_s, gmlp_b_s, head_norm_g, w_out, w_ff1, w_ff2):
    depth = ada_w.shape[0]
    n_p, t_p, d = x_prompt.shape
    n_s, t_s, _ = x_sample.shape
    n_phys = cache_fox_k.shape[1]
    rows_s = n_s * t_s
    gw, gh = GROUP_WIDTH, GROUP_HEADS

    mods = _modulation(jnp.concatenate([c_prompt, c_sample], axis=0), ada_w, ada_b)
    mods_p = mods[:, :n_p].reshape(depth, n_p, 1, N_MOD, d)
    mods_s = jnp.repeat(mods[:, n_p:], t_s, axis=1).reshape(depth, 1, rows_s, N_MOD, d)

    n_pages = page_table.shape[1]
    cfk, cfv = _pages_transposed(cache_fox_k), _pages_transposed(cache_fox_v)
    csk, csv = _pages_transposed(cache_sb_k), _pages_transposed(cache_sb_v)
    clogf = jnp.swapaxes(cache_fox_logf, 2, 3)
    later = _later_pages_matrix(n_pages)

    zeros_cb = jnp.zeros((n_p, 8, GDN_CONV_DIM), F32)
    zeros_s0 = jnp.zeros((n_p, gh, HEAD_DIM, HEAD_DIM), F32)
    eye_s = jnp.eye(n_s, dtype=F32)

    w_all, wt_all = _permute_w_in(w_in)
    wo_all, w1_all, w2_all = w_out.astype(BF16), w_ff1.astype(BF16), w_ff2.astype(BF16)

    x_p = x_prompt
    x_s = x_sample.reshape(1, rows_s, d)
    outs_p, outs_s = [], []
    for l in range(depth):
        sp = _small_params(b_forget[l], gdn_a_log[l], gdn_dt_bias[l])
        hg = head_norm_g[l].reshape(4, 1, gw)
        ng = norm_g[l]
        bias_p = jnp.repeat(gmlp_b_s[l].T, HEAD_DIM, axis=1)
        vg = gmlp_v_norm_g[l].reshape(1, gw)

        mp = [mods_p[l, :, :, i] for i in range(N_MOD)]
        fq, fk, fv, sq, sk, sv, gqkv, ggate, guv, small = _in_projection(
            x_p, mp[0], mp[1], ng[0:1], w_all, wt_all, sp, layer=l, cumsum=True, kv_transposed=True)
        frow = jnp.swapaxes(small[:, :, SM_CUM:SM_CUM + gh], 1, 2)
        o_fox = _fox_prompt(fq, fk, fv, small, frow, hg[0])
        o_sb = _sb_prompt(sq, sk, sv, hg[1])
        o_gdn, s_new, cb_new = _gated_deltanet(gqkv, small, ggate, zeros_cb, zeros_s0, gdn_conv_w[l], hg[2],
                                               c=GDN_CHUNK, n_valid=None)
        o_gmlp, _ = _gmlp(guv, vg, gmlp_w_s[l], bias_p, hg[3])
        x_p = _post(o_fox, o_sb, o_gdn, o_gmlp, x_p, mp[2], mp[3], mp[4], mp[5], ng, wo_all, w1_all, w2_all, layer=l)
        outs_p.append((_heads_last(fk, t_p), _heads_last(fv, t_p), small[:, :, SM_LOGF:SM_LOGF + gh],
                       _heads_last(sk, t_p), _heads_last(sv, t_p), s_new, cb_new[:, 8 - (CONV_WIDTH - 1):]))

        ms = [mods_s[l, :, :, i] for i in range(N_MOD)]
        fq, fk, fv, sq, sk, sv, gqkv, ggate, guv, small = _in_projection(
            x_s, ms[0], ms[1], ng[0:1], w_all, wt_all, sp, layer=l, cumsum=False, kv_transposed=False)
        logf_new = small[0, :, SM_LOGF:SM_LOGF + gh]
        new_rows = jnp.pad(jnp.swapaxes(logf_new.reshape(n_s, t_s, gh), 1, 2),
                           ((0, 0), (0, gh), (0, PAGE_SIZE - t_s)))
        bias = _dec_bias(page_table, clogf, new_rows, later, layer=l)
        per_seq = lambda a: a.reshape(n_s, t_s, gw)
        pad8 = lambda a: jnp.pad(per_seq(a), ((0, 0), (0, 8 - t_s), (0, 0)))
        tile_q = lambda a: jnp.tile(per_seq(a), (1, gh, 1))
        o_fox, o_sb = _dec_attention(page_table, cfk, cfv, csk, csv, tile_q(fq), tile_q(sq),
                                     pad8(fk), pad8(fv), pad8(sk), pad8(sv), bias, hg[0], hg[1],
                                     layer=l, n_q=t_s)
        cs = GDN_CHUNK_SMALL
        padc = lambda a: jnp.pad(a.reshape(n_s, t_s, a.shape[-1]), ((0, 0), (0, cs - t_s), (0, 0)))
        cb8 = jnp.pad(state_gdn_conv[l], ((0, 0), (8 - (CONV_WIDTH - 1), 0), (0, 0)))
        o_gdn, s_new, cb_new = _gated_deltanet(padc(gqkv), padc(small), padc(ggate), cb8, state_gdn[l],
                                               gdn_conv_w[l], hg[2], c=cs, n_valid=t_s)
        o_gdn = o_gdn[:, :t_s].reshape(1, rows_s, gw)
        w_blk = jnp.einsum('ab,gij->gaibj', eye_s, gmlp_w_s[l][:, :t_s, :t_s]).reshape(gh, rows_s, rows_s)
        bias_s = jnp.tile(bias_p[:t_s], (n_s, 1))
        o_gmlp, v_rows = _gmlp(guv, vg, w_blk, bias_s, hg[3])
        x_s = _post(o_fox.reshape(1, rows_s, gw), o_sb.reshape(1, rows_s, gw), o_gdn, o_gmlp, x_s,
                    ms[2], ms[3], ms[4], ms[5], ng, wo_all, w1_all, w2_all, layer=l)
        outs_s.append((per_seq(fk[0]), per_seq(fv[0]), logf_new.reshape(n_s, t_s, gh), per_seq(sk[0]),
                       per_seq(sv[0]), s_new, cb_new[:, 8 - (CONV_WIDTH - 1):], per_seq(v_rows[0])))

    def stacked(states, i, shape=None):
        a = jnp.stack([s[i] for s in states])
        return a if shape is None else a.reshape(shape)

    hp = (depth, n_p, t_p, gh, HEAD_DIM)
    hs = (depth, n_s, t_s, gh, HEAD_DIM)
    return (x_p, x_s.reshape(n_s, t_s, d),
            stacked(outs_p, 0, hp), stacked(outs_p, 1, hp), stacked(outs_p, 2), stacked(outs_p, 3, hp),
            stacked(outs_p, 4, hp), stacked(outs_p, 5), stacked(outs_p, 6),
            stacked(outs_s, 0, hs), stacked(outs_s, 1, hs), stacked(outs_s, 2), stacked(outs_s, 3, hs),
            stacked(outs_s, 4, hs), stacked(outs_s, 5), stacked(outs_s, 6), stacked(outs_s, 7))
```
